```python
import math
import jax, jax.numpy as jnp
from jax import lax
import numpy as np

D_MODEL = 2048
BATCH = 1
SEQ = 16384
DEPTH = 2

HEAD_DIM = 128
N_Q_HEADS = 16
N_KV_HEADS = 4
ATTN_WIDTH = N_Q_HEADS * HEAD_DIM
KV_WIDTH = N_KV_HEADS * HEAD_DIM
CONV_WIDTH = 1024
CONV_K = 3
GMLP_WIDTH = 1024
GMLP_GROUPS = 8
GMLP_GROUP_DIM = GMLP_WIDTH // GMLP_GROUPS
CHUNK = 128
BLOCK = 128
WINDOW = 128
ROPE_THETA = 500000.0
ROPE_DIM = HEAD_DIM // 4
N_BRANCHES = 3
LN_EPS = 1e-5
ALPHA = (2.0 * DEPTH) ** 0.25
BETA = (8.0 * DEPTH) ** -0.25

SEG_WIDTHS = (CONV_WIDTH,) * 4 + (GMLP_WIDTH,) * 3 + (ATTN_WIDTH, KV_WIDTH, KV_WIDTH, ATTN_WIDTH) + (D_MODEL,) * N_BRANCHES
SEG_OFFSETS = tuple(int(o) for o in np.cumsum((0,) + SEG_WIDTHS[:-1]))
IN_WIDTH = int(sum(SEG_WIDTHS))

kernel_name = "hybrid_conv_gmlp_swa_encoder"


def layer_norm(x, g, b):
    xf = x.astype(jnp.float32)
    mu = jnp.mean(xf, axis=-1, keepdims=True)
    var = jnp.mean(jnp.square(xf - mu), axis=-1, keepdims=True)
    return ((xf - mu) * lax.rsqrt(var + LN_EPS)).astype(x.dtype) * g + b


def partial_rope(t, positions):
    half = ROPE_DIM // 2
    inv_freq = ROPE_THETA ** (-jnp.arange(half, dtype=jnp.float32) / half)
    ang = positions.astype(jnp.float32)[..., None] * inv_freq
    cos = jnp.cos(ang)[:, :, None, :].astype(t.dtype)
    sin = jnp.sin(ang)[:, :, None, :].astype(t.dtype)
    t1, t2, rest = t[..., :half], t[..., half:ROPE_DIM], t[..., ROPE_DIM:]
    return jnp.concatenate([t1 * cos - t2 * sin, t2 * cos + t1 * sin, rest], axis=-1)


def short_conv_mixer(b_gate, c_gate, h, conv_w):
    y = c_gate * h
    s = y.shape[1]
    yp = jnp.pad(y, ((0, 0), (1, 1), (0, 0)))
    conv = conv_w[0] * yp[:, :s] + conv_w[1] * yp[:, 1:s + 1] + conv_w[2] * yp[:, 2:]
    return b_gate * conv


def chunked_spatial_gating(u, v, ln_g, ln_b, w_s, b_s):
    u = jax.nn.gelu(u)
    v = layer_norm(jax.nn.gelu(v), ln_g, ln_b)
    bsz, s, _ = v.shape
    vc = v.reshape(bsz, s // CHUNK, CHUNK, GMLP_GROUPS, GMLP_GROUP_DIM)
    mixed = jnp.einsum('gpq,bnqgc->bnpgc', w_s, vc) + b_s.T[None, None, :, :, None]
    return u * mixed.reshape(bsz, s, GMLP_WIDTH)


def windowed_sink_attention(q, k, v, sink, positions):
    bsz, s, _ = q.shape
    nb = s // BLOCK
    grp = N_Q_HEADS // N_KV_HEADS
    q = partial_rope(q.reshape(bsz, s, N_Q_HEADS, HEAD_DIM), positions)
    k = partial_rope(k.reshape(bsz, s, N_KV_HEADS, HEAD_DIM), positions)
    v = v.reshape(bsz, s, N_KV_HEADS, HEAD_DIM)
    qb = q.reshape(bsz, nb, BLOCK, N_KV_HEADS, grp, HEAD_DIM)

    def band(t):
        tb = t.reshape(bsz, nb, BLOCK, N_KV_HEADS, HEAD_DIM)
        tp = jnp.pad(tb, ((0, 0), (1, 1), (0, 0), (0, 0), (0, 0)))
        return jnp.concatenate([tp[:, :-2], tp[:, 1:-1], tp[:, 2:]], axis=2)

    kb, vb = band(k), band(v)
    scores = jnp.einsum('bnqhgd,bnkhd->bnhgqk', qb, kb).astype(jnp.float32) * (HEAD_DIM ** -0.5)
    blk = jnp.arange(nb)[:, None, None]
    qpos = blk * BLOCK + jnp.arange(BLOCK)[None, :, None]
    kpos = (blk - 1) * BLOCK + jnp.arange(3 * BLOCK)[None, None, :]
    valid = (jnp.abs(qpos - kpos) <= WINDOW) & (kpos >= 0) & (kpos < s)
    scores = jnp.where(valid[None, :, None, None], scores, -jnp.inf)
    sink_l = sink.astype(jnp.float32).reshape(N_KV_HEADS, grp)[None, None, :, :, None, None]
    m = jnp.maximum(jnp.max(scores, axis=-1, keepdims=True), sink_l)
    p = jnp.exp(scores - m)
    denom = jnp.sum(p, axis=-1, keepdims=True) + jnp.exp(sink_l - m)
    probs = (p / denom).astype(vb.dtype)
    out = jnp.einsum('bnhgqk,bnkhd->bnqhgd', probs, vb)
    return out.reshape(bsz, s, ATTN_WIDTH)


def hybrid_layer(x, positions, w_in, conv_w, gmlp_ln_g, gmlp_ln_b, spatial_w, spatial_b, sink,
                 w_branch_a, w_branch_b, w_branch_c, gate_b, w_out, ln_g, ln_b):
    (a_b, a_c, a_h, a_z, g_u, g_v, g_z, q, k, v, c_z, r_a, r_b, r_c) = [
        jnp.einsum('bsd,de->bse', x, w_in[:, o:o + w]) for o, w in zip(SEG_OFFSETS, SEG_WIDTHS)]
    y_a = short_conv_mixer(a_b, a_c, a_h, conv_w) * jax.nn.silu(a_z)
    y_b = chunked_spatial_gating(g_u, g_v, gmlp_ln_g, gmlp_ln_b, spatial_w, spatial_b) * jax.nn.silu(g_z)
    y_c = windowed_sink_attention(q, k, v, sink, positions) * jax.nn.silu(c_z)
    merged = (jax.nn.sigmoid(r_a + gate_b[0]) * jnp.einsum('bsc,cd->bsd', y_a, w_branch_a)
              + jax.nn.sigmoid(r_b + gate_b[1]) * jnp.einsum('bsc,cd->bsd', y_b, w_branch_b)
              + jax.nn.sigmoid(r_c + gate_b[2]) * jnp.einsum('bsc,cd->bsd', y_c, w_branch_c))
    out = jnp.einsum('bsd,de->bse', merged, w_out)
    return layer_norm(ALPHA * x + out, ln_g, ln_b)


def setup_inputs(seed: int = 0) -> dict:
    key = jax.random.key(seed)
    ks = jax.random.split(key, 18)
    f32 = jnp.float32
    nrm = lambda k, shape, scale: jax.random.normal(k, shape, f32) * scale
    L = DEPTH
    x = jax.random.normal(ks[0], (BATCH, SEQ, D_MODEL), f32)
    offset = jax.random.randint(ks[1], (BATCH, 1), 0, 1024, dtype=jnp.int32)
    positions = (jnp.arange(SEQ, dtype=jnp.int32)[None, :] + offset).astype(jnp.int32)
    return {
        "x": x,
        "positions": positions,
        "ln0_g": 1.0 + nrm(ks[2], (D_MODEL,), 0.02),
        "ln0_b": nrm(ks[3], (D_MODEL,), 0.02),
        "w_in": nrm(ks[4], (L, D_MODEL, IN_WIDTH), D_MODEL ** -0.5),
        "conv_w": nrm(ks[5], (L, CONV_K, CONV_WIDTH), CONV_K ** -0.5),
        "gmlp_ln_g": 1.0 + nrm(ks[6], (L, GMLP_WIDTH), 0.02),
        "gmlp_ln_b": nrm(ks[7], (L, GMLP_WIDTH), 0.02),
        "spatial_w": nrm(ks[8], (L, GMLP_GROUPS, CHUNK, CHUNK), CHUNK ** -0.5),
        "spatial_b": 1.0 + nrm(ks[9], (L, GMLP_GROUPS, CHUNK), 0.02),
        "sink": nrm(ks[10], (L, N_Q_HEADS), 0.5),
        "w_branch_a": nrm(ks[11], (L, CONV_WIDTH, D_MODEL), BETA * CONV_WIDTH ** -0.5),
        "w_branch_b": nrm(ks[12], (L, GMLP_WIDTH, D_MODEL), BETA * GMLP_WIDTH ** -0.5),
        "w_branch_c": nrm(ks[13], (L, ATTN_WIDTH, D_MODEL), BETA * ATTN_WIDTH ** -0.5),
        "gate_b": nrm(ks[14], (L, N_BRANCHES, D_MODEL), 0.02),
        "w_out": nrm(ks[15], (L, D_MODEL, D_MODEL), BETA * D_MODEL ** -0.5),
        "ln_g": 1.0 + nrm(ks[16], (L, D_MODEL), 0.02),
        "ln_b": nrm(ks[17], (L, D_MODEL), 0.02),
    }


def reference(x, positions, ln0_g, ln0_b, w_in, conv_w, gmlp_ln_g, gmlp_ln_b, spatial_w, spatial_b,
              sink, w_branch_a, w_branch_b, w_branch_c, gate_b, w_out, ln_g, ln_b):
    h = layer_norm(x, ln0_g, ln0_b)
    for l in range(DEPTH):
        h = hybrid_layer(h, positions, w_in[l], conv_w[l], gmlp_ln_g[l], gmlp_ln_b[l], spatial_w[l],
                         spatial_b[l], sink[l], w_branch_a[l], w_branch_b[l], w_branch_c[l], gate_b[l],
                         w_out[l], ln_g[l], ln_b[l])
    return h
```

```python
import functools

import jax
import jax.numpy as jnp
from jax import lax
from jax.experimental import pallas as pl
from jax.experimental.pallas import tpu as pltpu

D_MODEL = 2048
SEQ = 16384
DEPTH = 2
HEAD_DIM = 128
N_Q_HEADS = 16
N_KV_HEADS = 4
GROUP = N_Q_HEADS // N_KV_HEADS
ATTN_WIDTH = N_Q_HEADS * HEAD_DIM
KV_WIDTH = N_KV_HEADS * HEAD_DIM
CONV_WIDTH = 1024
GMLP_WIDTH = 1024
GMLP_GROUPS = 8
GMLP_GROUP_DIM = GMLP_WIDTH // GMLP_GROUPS
CHUNK = 128
BLOCK = 128
ROPE_THETA = 500000.0
ROPE_DIM = HEAD_DIM // 4
ROPE_HALF = ROPE_DIM // 2
LN_EPS = 1e-5
ALPHA = (2.0 * DEPTH) ** 0.25
IN_WIDTH = 4 * CONV_WIDTH + 3 * GMLP_WIDTH + 2 * ATTN_WIDTH + 2 * KV_WIDTH + 3 * D_MODEL

COL_CONV_B, COL_CONV_C, COL_CONV_H, COL_CONV_Z = 0, 1, 2, 3
COL_GMLP_U, COL_GMLP_V, COL_GMLP_Z = 4, 5, 6
COL_Q_LO, COL_Q_HI = 7, 8
COL_K, COL_V = 18, 19
COL_ATTN_Z = 5
COL_GATE_A, COL_GATE_B, COL_GATE_C = 6, 7, 8

VMEM_LIMIT_BYTES = 56 * 1024 * 1024
BF16_SUBLANES = 16

F32 = jnp.float32
BF16 = jnp.bfloat16


def _params(*semantics):
    return pltpu.CompilerParams(dimension_semantics=semantics, vmem_limit_bytes=VMEM_LIMIT_BYTES)


def _silu(x):
    return x / (1.0 + jnp.exp(-x))


def _sigmoid(x):
    return 1.0 / (1.0 + jnp.exp(-x))


def _gelu_tanh(x):
    return 0.5 * x * (1.0 + jnp.tanh(0.7978845608028654 * (x + 0.044715 * (x * x * x))))


def _layer_norm(x, g, b):
    mu = jnp.mean(x, axis=-1, keepdims=True)
    xc = x - mu
    var = jnp.mean(xc * xc, axis=-1, keepdims=True)
    return xc * lax.rsqrt(var + LN_EPS) * g + b


def _ln_kernel(x_ref, g_ref, b_ref, of_ref, ob_ref):
    y = _layer_norm(x_ref[...], g_ref[...], b_ref[...])
    of_ref[...] = y
    ob_ref[...] = y.astype(BF16)


def _input_layer_norm(x, g, b, rows=512):
    s, d = x.shape
    row_spec = pl.BlockSpec((rows, d), lambda i: (i, 0))
    vec_spec = pl.BlockSpec((1, d), lambda i: (0, 0))
    return pl.pallas_call(
        _ln_kernel,
        grid=(s // rows,),
        in_specs=[row_spec, vec_spec, vec_spec],
        out_specs=[row_spec, row_spec],
        out_shape=[jax.ShapeDtypeStruct((s, d), F32), jax.ShapeDtypeStruct((s, d), BF16)],
        compiler_params=_params("parallel"),
        name="input_layer_norm",
    )(x, g.reshape(1, d), b.reshape(1, d))


def _rope_table_kernel(pos_ref, invf_ref, cos_ref, sin_ref):
    ang = pos_ref[...].astype(F32) * invf_ref[...]
    lane = lax.broadcasted_iota(jnp.int32, ang.shape, 1)
    c = jnp.cos(ang)
    s = jnp.sin(ang)
    cos_ref[...] = jnp.where(lane < ROPE_DIM, c, 1.0)
    sin_ref[...] = jnp.where(lane < ROPE_HALF, -s, jnp.where(lane < ROPE_DIM, s, 0.0))


def _rope_tables(positions, rows=1024):
    s = positions.shape[0]
    inv_freq = ROPE_THETA ** (-jnp.arange(ROPE_HALF, dtype=F32) / ROPE_HALF)
    invf = jnp.tile(inv_freq, HEAD_DIM // ROPE_HALF).reshape(1, HEAD_DIM)
    tab = jax.ShapeDtypeStruct((s, HEAD_DIM), F32)
    tab_spec = pl.BlockSpec((rows, HEAD_DIM), lambda i: (i, 0))
    return pl.pallas_call(
        _rope_table_kernel,
        grid=(s // rows,),
        in_specs=[pl.BlockSpec((rows, 1), lambda i: (i, 0)), pl.BlockSpec((1, HEAD_DIM), lambda i: (0, 0))],
        out_specs=[tab_spec, tab_spec],
        out_shape=[tab, tab],
        compiler_params=_params("parallel"),
        name="rope_tables",
    )(positions.reshape(s, 1), invf)


def _matmul_kernel(x_ref, w_ref, o_ref):
    o_ref[...] = jnp.dot(x_ref[...], w_ref[...], preferred_element_type=F32).astype(o_ref.dtype)


def _input_projection(h_bf16, w_bf16, tm=1024, tn=1024):
    m, k = h_bf16.shape
    n = w_bf16.shape[1]
    return pl.pallas_call(
        _matmul_kernel,
        grid=(m // tm, n // tn),
        in_specs=[pl.BlockSpec((tm, k), lambda i, j: (i, 0)), pl.BlockSpec((k, tn), lambda i, j: (0, j))],
        out_specs=pl.BlockSpec((tm, tn), lambda i, j: (i, j)),
        out_shape=jax.ShapeDtypeStruct((m, n), BF16),
        compiler_params=_params("parallel", "parallel"),
        name="input_projection",
    )(h_bf16, w_bf16)


def _conv_kernel(b_ref, c_ref, h_ref, z_ref, cp_ref, hp_ref, cn_ref, hn_ref, w_ref, o_ref, *, rows):
    i = pl.program_id(0)
    last = pl.num_programs(0) - 1
    y = c_ref[...].astype(F32) * h_ref[...].astype(F32)
    y_prev = cp_ref[BF16_SUBLANES - 1:, :].astype(F32) * hp_ref[BF16_SUBLANES - 1:, :].astype(F32)
    y_next = cn_ref[:1, :].astype(F32) * hn_ref[:1, :].astype(F32)
    y_prev = jnp.where(i > 0, y_prev, 0.0)
    y_next = jnp.where(i < last, y_next, 0.0)
    row = lax.broadcasted_iota(jnp.int32, y.shape, 0)
    up = jnp.where(row == 0, y_prev, pltpu.roll(y, 1, 0))
    dn = jnp.where(row == rows - 1, y_next, pltpu.roll(y, rows - 1, 0))
    w = w_ref[...]
    conv = w[0:1, :] * up + w[1:2, :] * y + w[2:3, :] * dn
    o_ref[...] = (b_ref[...].astype(F32) * conv * _silu(z_ref[...].astype(F32))).astype(o_ref.dtype)


def _conv_mixer(p, conv_w, rows=512):
    s = p.shape[0]
    halo_blocks = rows // BF16_SUBLANES
    n_halo = s // BF16_SUBLANES

    def main(col):
        return pl.BlockSpec((rows, CONV_WIDTH), lambda i: (i, col))

    def prev(col):
        return pl.BlockSpec((BF16_SUBLANES, CONV_WIDTH), lambda i: (jnp.maximum(i * halo_blocks - 1, 0), col))

    def nxt(col):
        return pl.BlockSpec((BF16_SUBLANES, CONV_WIDTH),
                            lambda i: (jnp.minimum((i + 1) * halo_blocks, n_halo - 1), col))

    return pl.pallas_call(
        functools.partial(_conv_kernel, rows=rows),
        grid=(s // rows,),
        in_specs=[main(COL_CONV_B), main(COL_CONV_C), main(COL_CONV_H), main(COL_CONV_Z),
                  prev(COL_CONV_C), prev(COL_CONV_H), nxt(COL_CONV_C), nxt(COL_CONV_H),
                  pl.BlockSpec((3, CONV_WIDTH), lambda i: (0, 0))],
        out_specs=pl.BlockSpec((rows, CONV_WIDTH), lambda i: (i, 0)),
        out_shape=jax.ShapeDtypeStruct((s, CONV_WIDTH), BF16),
        compiler_params=_params("parallel"),
        name="conv_mixer",
    )(p, p, p, p, p, p, p, p, conv_w)


def _gmlp_kernel(u_ref, v_ref, z_ref, g_ref, b_ref, ws_ref, bias_ref, o_ref, *, rows):
    v = _layer_norm(_gelu_tanh(v_ref[...].astype(F32)), g_ref[...], b_ref[...]).astype(BF16)
    for c in range(rows // CHUNK):
        r = slice(c * CHUNK, (c + 1) * CHUNK)
        for g in range(GMLP_GROUPS):
            cols = slice(g * GMLP_GROUP_DIM, (g + 1) * GMLP_GROUP_DIM)
            mixed = jnp.dot(ws_ref[g], v[r, cols], preferred_element_type=F32) + bias_ref[:, cols]
            u = _gelu_tanh(u_ref[r, cols].astype(F32))
            o_ref[r, cols] = (u * mixed * _silu(z_ref[r, cols].astype(F32))).astype(o_ref.dtype)


def _gmlp_mixer(p, ln_g, ln_b, spatial_w, spatial_b, rows=256):
    s = p.shape[0]
    bias = jnp.repeat(spatial_b.T, GMLP_GROUP_DIM, axis=1)

    def main(col):
        return pl.BlockSpec((rows, GMLP_WIDTH), lambda i: (i, col))

    vec = pl.BlockSpec((1, GMLP_WIDTH), lambda i: (0, 0))
    return pl.pallas_call(
        functools.partial(_gmlp_kernel, rows=rows),
        grid=(s // rows,),
        in_specs=[main(COL_GMLP_U), main(COL_GMLP_V), main(COL_GMLP_Z), vec, vec,
                  pl.BlockSpec((GMLP_GROUPS, CHUNK, CHUNK), lambda i: (0, 0, 0)),
                  pl.BlockSpec((CHUNK, GMLP_WIDTH), lambda i: (0, 0))],
        out_specs=pl.BlockSpec((rows, GMLP_WIDTH), lambda i: (i, 0)),
        out_shape=jax.ShapeDtypeStruct((s, GMLP_WIDTH), BF16),
        compiler_params=_params("parallel"),
        name="gmlp_mixer",
    )(p, p, p, ln_g.reshape(1, -1), ln_b.reshape(1, -1), spatial_w.astype(BF16), bias)


def _rope(t, cos, sin):
    lane = lax.broadcasted_iota(jnp.int32, t.shape, 1)
    partner = jnp.where(lane < ROPE_HALF, pltpu.roll(t, HEAD_DIM - ROPE_HALF, 1), pltpu.roll(t, ROPE_HALF, 1))
    return t * cos + partner * sin


def _attn_kernel(sink_ref, qlo_ref, qhi_ref, k_ref, v_ref, z_ref, kp_ref, vp_ref, kn_ref, vn_ref,
                 cos_ref, sin_ref, cosp_ref, sinp_ref, cosn_ref, sinn_ref, o_ref, *, rows):
    i = pl.program_id(0)
    last = pl.num_programs(0) - 1
    n_blocks = rows // BLOCK
    cos_all = jnp.concatenate([cosp_ref[...], cos_ref[...], cosn_ref[...]], axis=0)
    sin_all = jnp.concatenate([sinp_ref[...], sin_ref[...], sinn_ref[...]], axis=0)
    k_all = jnp.concatenate([kp_ref[...], k_ref[...], kn_ref[...]], axis=0).astype(F32)
    v_all = jnp.concatenate([vp_ref[...], v_ref[...], vn_ref[...]], axis=0)
    cos_q = cos_ref[...]
    sin_q = sin_ref[...]
    scale = HEAD_DIM ** -0.5

    q_row = lax.broadcasted_iota(jnp.int32, (GROUP * BLOCK, 3 * BLOCK), 0) % BLOCK
    k_col = lax.broadcasted_iota(jnp.int32, (GROUP * BLOCK, 3 * BLOCK), 1)
    band = (k_col >= q_row) & (k_col <= q_row + 2 * BLOCK)
    head_of_row = lax.broadcasted_iota(jnp.int32, (GROUP * BLOCK, 1), 0) // BLOCK

    for g in range(N_KV_HEADS):
        kv_cols = slice(g * HEAD_DIM, (g + 1) * HEAD_DIM)
        k_rot = _rope(k_all[:, kv_cols], cos_all, sin_all).astype(BF16)
        v_g = v_all[:, kv_cols]
        sink_col = jnp.zeros((GROUP * BLOCK, 1), F32)
        for j in range(GROUP):
            sink_col = jnp.where(head_of_row == j, sink_ref[g * GROUP + j], sink_col)
        q_heads = []
        for j in range(GROUP):
            head = g * GROUP + j
            q_ref = qlo_ref if head < N_Q_HEADS // 2 else qhi_ref
            c0 = (head % (N_Q_HEADS // 2)) * HEAD_DIM
            q = q_ref[:, c0:c0 + HEAD_DIM].astype(F32)
            q_heads.append((_rope(q, cos_q, sin_q) * scale).astype(BF16))
        for b in range(n_blocks):
            r = slice(b * BLOCK, (b + 1) * BLOCK)
            kb = k_rot[b * BLOCK:(b + 3) * BLOCK]
            vb = v_g[b * BLOCK:(b + 3) * BLOCK]
            q4 = jnp.concatenate([qh[r] for qh in q_heads], axis=0)
            scores = lax.dot_general(q4, kb, (((1,), (1,)), ((), ())), preferred_element_type=F32)
            valid = band
            if b == 0:
                valid = valid & (k_col >= jnp.where(i == 0, BLOCK, 0))
            if b == n_blocks - 1:
                valid = valid & (k_col < jnp.where(i == last, 2 * BLOCK, 3 * BLOCK))
            scores = jnp.where(valid, scores, -jnp.inf)
            m = jnp.maximum(jnp.max(scores, axis=-1, keepdims=True), sink_col)
            p = jnp.exp(scores - m)
            denom = jnp.sum(p, axis=-1, keepdims=True) + jnp.exp(sink_col - m)
            out = jnp.dot(p.astype(BF16), vb, preferred_element_type=F32) / denom
            for j in range(GROUP):
                c0 = (g * GROUP + j) * HEAD_DIM
                zj = z_ref[r, c0:c0 + HEAD_DIM].astype(F32)
                o_ref[r, c0:c0 + HEAD_DIM] = (out[j * BLOCK:(j + 1) * BLOCK] * _silu(zj)).astype(o_ref.dtype)


def _attention_mixer(p, sink, cos_tab, sin_tab, rows=256):
    s = p.shape[0]
    blocks = rows // BLOCK
    n_blocks = s // BLOCK

    def own(width, col):
        return pl.BlockSpec((rows, width), lambda i: (i, col))

    def prev(width, col):
        return pl.BlockSpec((BLOCK, width), lambda i: (jnp.maximum(i * blocks - 1, 0), col))

    def nxt(width, col):
        return pl.BlockSpec((BLOCK, width), lambda i: (jnp.minimum((i + 1) * blocks, n_blocks - 1), col))

    half_q = ATTN_WIDTH // 2
    return pl.pallas_call(
        functools.partial(_attn_kernel, rows=rows),
        grid=(s // rows,),
        in_specs=[pl.BlockSpec(memory_space=pltpu.SMEM),
                  own(half_q, COL_Q_LO), own(half_q, COL_Q_HI), own(KV_WIDTH, COL_K), own(KV_WIDTH, COL_V),
                  own(ATTN_WIDTH, COL_ATTN_Z),
                  prev(KV_WIDTH, COL_K), prev(KV_WIDTH, COL_V), nxt(KV_WIDTH, COL_K), nxt(KV_WIDTH, COL_V),
                  own(HEAD_DIM, 0), own(HEAD_DIM, 0), prev(HEAD_DIM, 0), prev(HEAD_DIM, 0),
                  nxt(HEAD_DIM, 0), nxt(HEAD_DIM, 0)],
        out_specs=pl.BlockSpec((rows, ATTN_WIDTH), lambda i: (i, 0)),
        out_shape=jax.ShapeDtypeStruct((s, ATTN_WIDTH), BF16),
        compiler_params=_params("parallel"),
        name="attention_mixer",
    )(sink, p, p, p, p, p, p, p, p, p, cos_tab, sin_tab, cos_tab, sin_tab, cos_tab, sin_tab)


def _merge_kernel(ya_ref, yb_ref, yc_ref, ra_ref, rb_ref, rc_ref, h_ref, wa_ref, wb_ref, wc_ref, wo_ref,
                  gb_ref, g_ref, b_ref, of_ref, ob_ref):
    gb = gb_ref[...]
    merged = _sigmoid(ra_ref[...].astype(F32) + gb[0:1, :]) * jnp.dot(
        ya_ref[...], wa_ref[...], preferred_element_type=F32)
    merged += _sigmoid(rb_ref[...].astype(F32) + gb[1:2, :]) * jnp.dot(
        yb_ref[...], wb_ref[...], preferred_element_type=F32)
    merged += _sigmoid(rc_ref[...].astype(F32) + gb[2:3, :]) * jnp.dot(
        yc_ref[...], wc_ref[...], preferred_element_type=F32)
    out = jnp.dot(merged.astype(BF16), wo_ref[...], preferred_element_type=F32)
    y = _layer_norm(ALPHA * h_ref[...] + out, g_ref[...], b_ref[...])
    of_ref[...] = y
    ob_ref[...] = y.astype(BF16)


def _merge_project_norm(y_a, y_b, y_c, p, h, w_a, w_b, w_c, w_o, gate_b, ln_g, ln_b, rows=256):
    s, d = h.shape

    def row(width, col=0):
        return pl.BlockSpec((rows, width), lambda i: (i, col))

    def resident(shape):
        return pl.BlockSpec(shape, lambda i: (0, 0), pipeline_mode=pl.Buffered(1))

    return pl.pallas_call(
        _merge_kernel,
        grid=(s // rows,),
        in_specs=[row(CONV_WIDTH), row(GMLP_WIDTH), row(ATTN_WIDTH),
                  row(d, COL_GATE_A), row(d, COL_GATE_B), row(d, COL_GATE_C), row(d),
                  resident((CONV_WIDTH, d)), resident((GMLP_WIDTH, d)), resident((ATTN_WIDTH, d)),
                  resident((d, d)), resident((3, d)), resident((1, d)), resident((1, d))],
        out_specs=[row(d), row(d)],
        out_shape=[jax.ShapeDtypeStruct((s, d), F32), jax.ShapeDtypeStruct((s, d), BF16)],
        compiler_params=_params("parallel"),
        name="merge_project_norm",
    )(y_a, y_b, y_c, p, p, p, h, w_a, w_b, w_c, w_o, gate_b, ln_g.reshape(1, d), ln_b.reshape(1, d))


def kernel(x, positions, ln0_g, ln0_b, w_in, conv_w, gmlp_ln_g, gmlp_ln_b, spatial_w, spatial_b, sink,
           w_branch_a, w_branch_b, w_branch_c, gate_b, w_out, ln_g, ln_b):
    bsz, s, d = x.shape
    assert (bsz, s, d) == (1, SEQ, D_MODEL) and w_in.shape == (DEPTH, D_MODEL, IN_WIDTH)
    h, h_bf16 = _input_layer_norm(x.reshape(s, d), ln0_g, ln0_b)
    cos_tab, sin_tab = _rope_tables(positions.reshape(s))
    for l in range(DEPTH):
        p = _input_projection(h_bf16, w_in[l].astype(BF16))
        y_a = _conv_mixer(p, conv_w[l])
        y_b = _gmlp_mixer(p, gmlp_ln_g[l], gmlp_ln_b[l], spatial_w[l], spatial_b[l])
        y_c = _attention_mixer(p, sink[l], cos_tab, sin_tab)
        h, h_bf16 = _merge_project_norm(
            y_a, y_b, y_c, p, h, w_branch_a[l].astype(BF16), w_branch_b[l].astype(BF16),
            w_branch_c[l].astype(BF16), w_out[l].astype(BF16), gate_b[l], ln_g[l], ln_b[l])
    return h.reshape(bsz, s, d)
```

```python
import functools

import jax
import jax.numpy as jnp
from jax import lax
from jax.experimental import pallas as pl
from jax.experimental.pallas import tpu as pltpu

D_MODEL = 2048
SEQ = 16384
DEPTH = 2
HEAD_DIM = 128
N_Q_HEADS = 16
N_KV_HEADS = 4
GROUP = N_Q_HEADS // N_KV_HEADS
ATTN_WIDTH = N_Q_HEADS * HEAD_DIM
KV_WIDTH = N_KV_HEADS * HEAD_DIM
CONV_WIDTH = 1024
GMLP_WIDTH = 1024
GMLP_GROUPS = 8
GMLP_GROUP_DIM = GMLP_WIDTH // GMLP_GROUPS
CHUNK = 128
BLOCK = 128
ROPE_THETA = 500000.0
ROPE_DIM = HEAD_DIM // 4
ROPE_HALF = ROPE_DIM // 2
LN_EPS = 1e-5
ALPHA = (2.0 * DEPTH) ** 0.25
IN_WIDTH = 4 * CONV_WIDTH + 3 * GMLP_WIDTH + 2 * ATTN_WIDTH + 2 * KV_WIDTH + 3 * D_MODEL

OFF_CONV, OFF_GMLP, OFF_Q, OFF_KV, OFF_ATTN_Z = 0, 4096, 7168, 9216, 10240
OFF_GATE_A, OFF_GATE_B, OFF_GATE_C = 12288, 14336, 16384

COL_CONV_B, COL_CONV_C, COL_CONV_H, COL_CONV_Z = 0, 1, 2, 3
COL_ATTN_Z, COL_GATE_A, COL_GATE_C = 2, 3, 4
COL_Q_LO, COL_Q_HI = 10, 11
COL_K, COL_V = 24, 25

VMEM_LIMIT_BYTES = 56 * 1024 * 1024
BF16_SUBLANES = 16

F32 = jnp.float32
BF16 = jnp.bfloat16


def _params(*semantics):
    return pltpu.CompilerParams(dimension_semantics=semantics, vmem_limit_bytes=VMEM_LIMIT_BYTES)


def _silu(x):
    return x / (1.0 + jnp.exp(-x))


def _sigmoid(x):
    return 1.0 / (1.0 + jnp.exp(-x))


def _gelu_tanh(x):
    return 0.5 * x * (1.0 + jnp.tanh(0.7978845608028654 * (x + 0.044715 * (x * x * x))))


def _layer_norm(x, g, b):
    mu = jnp.mean(x, axis=-1, keepdims=True)
    xc = x - mu
    var = jnp.mean(xc * xc, axis=-1, keepdims=True)
    return xc * lax.rsqrt(var + LN_EPS) * g + b


def _ln_kernel(x_ref, g_ref, b_ref, of_ref, ob_ref):
    y = _layer_norm(x_ref[...], g_ref[...], b_ref[...])
    of_ref[...] = y
    ob_ref[...] = y.astype(BF16)


def _input_layer_norm(x, g, b, rows=512):
    s, d = x.shape
    row_spec = pl.BlockSpec((rows, d), lambda i: (i, 0))
    vec_spec = pl.BlockSpec((1, d), lambda i: (0, 0))
    return pl.pallas_call(
        _ln_kernel,
        grid=(s // rows,),
        in_specs=[row_spec, vec_spec, vec_spec],
        out_specs=[row_spec, row_spec],
        out_shape=[jax.ShapeDtypeStruct((s, d), F32), jax.ShapeDtypeStruct((s, d), BF16)],
        compiler_params=_params("parallel"),
        name="input_layer_norm",
    )(x, g.reshape(1, d), b.reshape(1, d))


def _rope_table_kernel(pos_ref, invf_ref, cos_ref, sin_ref):
    ang = pos_ref[...].astype(F32) * invf_ref[...]
    lane = lax.broadcasted_iota(jnp.int32, ang.shape, 1)
    c = jnp.cos(ang)
    s = jnp.sin(ang)
    cos_ref[...] = jnp.where(lane < ROPE_DIM, c, 1.0)
    sin_ref[...] = jnp.where(lane < ROPE_HALF, -s, jnp.where(lane < ROPE_DIM, s, 0.0))


def _rope_tables(positions, rows=1024):
    s = positions.shape[0]
    inv_freq = ROPE_THETA ** (-jnp.arange(ROPE_HALF, dtype=F32) / ROPE_HALF)
    invf = jnp.tile(inv_freq, HEAD_DIM // ROPE_HALF).reshape(1, HEAD_DIM)
    tab = jax.ShapeDtypeStruct((s, HEAD_DIM), F32)
    tab_spec = pl.BlockSpec((rows, HEAD_DIM), lambda i: (i, 0))
    return pl.pallas_call(
        _rope_table_kernel,
        grid=(s // rows,),
        in_specs=[pl.BlockSpec((rows, 1), lambda i: (i, 0)), pl.BlockSpec((1, HEAD_DIM), lambda i: (0, 0))],
        out_specs=[tab_spec, tab_spec],
        out_shape=[tab, tab],
        compiler_params=_params("parallel"),
        name="rope_tables",
    )(positions.reshape(s, 1), invf)


def _matmul_kernel(x_ref, w_ref, o_ref):
    o_ref[...] = jnp.dot(x_ref[...], w_ref[...], preferred_element_type=F32).astype(o_ref.dtype)


def _input_projection(h_bf16, w_bf16, tm=1024, tn=1024):
    m, k = h_bf16.shape
    n = w_bf16.shape[1]
    return pl.pallas_call(
        _matmul_kernel,
        grid=(m // tm, n // tn),
        in_specs=[pl.BlockSpec((tm, k), lambda i, j: (i, 0)), pl.BlockSpec((k, tn), lambda i, j: (0, j))],
        out_specs=pl.BlockSpec((tm, tn), lambda i, j: (i, j)),
        out_shape=jax.ShapeDtypeStruct((m, n), BF16),
        compiler_params=_params("parallel", "parallel"),
        name="input_projection",
    )(h_bf16, w_bf16)


def _conv_kernel(b_ref, c_ref, h_ref, z_ref, cp_ref, hp_ref, cn_ref, hn_ref, w_ref, o_ref, *, rows):
    i = pl.program_id(0)
    last = pl.num_programs(0) - 1
    y = c_ref[...].astype(F32) * h_ref[...].astype(F32)
    y_prev = cp_ref[BF16_SUBLANES - 1:, :].astype(F32) * hp_ref[BF16_SUBLANES - 1:, :].astype(F32)
    y_next = cn_ref[:1, :].astype(F32) * hn_ref[:1, :].astype(F32)
    y_prev = jnp.where(i > 0, y_prev, 0.0)
    y_next = jnp.where(i < last, y_next, 0.0)
    row = lax.broadcasted_iota(jnp.int32, y.shape, 0)
    up = jnp.where(row == 0, y_prev, pltpu.roll(y, 1, 0))
    dn = jnp.where(row == rows - 1, y_next, pltpu.roll(y, rows - 1, 0))
    w = w_ref[...]
    conv = w[0:1, :] * up + w[1:2, :] * y + w[2:3, :] * dn
    o_ref[...] = (b_ref[...].astype(F32) * conv * _silu(z_ref[...].astype(F32))).astype(o_ref.dtype)


def _conv_mixer(p, conv_w, rows=512):
    s = p.shape[0]
    halo_blocks = rows // BF16_SUBLANES
    n_halo = s // BF16_SUBLANES

    def main(col):
        return pl.BlockSpec((rows, CONV_WIDTH), lambda i: (i, col))

    def prev(col):
        return pl.BlockSpec((BF16_SUBLANES, CONV_WIDTH), lambda i: (jnp.maximum(i * halo_blocks - 1, 0), col))

    def nxt(col):
        return pl.BlockSpec((BF16_SUBLANES, CONV_WIDTH),
                            lambda i: (jnp.minimum((i + 1) * halo_blocks, n_halo - 1), col))

    return pl.pallas_call(
        functools.partial(_conv_kernel, rows=rows),
        grid=(s // rows,),
        in_specs=[main(COL_CONV_B), main(COL_CONV_C), main(COL_CONV_H), main(COL_CONV_Z),
                  prev(COL_CONV_C), prev(COL_CONV_H), nxt(COL_CONV_C), nxt(COL_CONV_H),
                  pl.BlockSpec((3, CONV_WIDTH), lambda i: (0, 0))],
        out_specs=pl.BlockSpec((rows, CONV_WIDTH), lambda i: (i, 0)),
        out_shape=jax.ShapeDtypeStruct((s, CONV_WIDTH), BF16),
        compiler_params=_params("parallel"),
        name="conv_mixer",
    )(p, p, p, p, p, p, p, p, conv_w)


GMLP_PROJ_WIDTH = 3 * GMLP_WIDTH + D_MODEL
PROJ_CHUNK = 1024
BRANCH_TILE = 256


def _gmlp_branch_kernel(h_ref, w_ref, wb_ref, g_ref, b_ref, ws_ref, bias_ref, gb_ref, o_ref,
                        p0_ref, p1_ref, y0_ref, y1_ref):
    tile = BRANCH_TILE

    @pl.when(pl.program_id(0) == 0)
    def _():
        p1_ref[...] = jnp.zeros_like(p1_ref)

    def project(rows, p_ref):
        h = h_ref[rows, :]
        for c in range(0, GMLP_PROJ_WIDTH, PROJ_CHUNK):
            p_ref[:, c:c + PROJ_CHUNK] = jnp.dot(
                h, w_ref[:, c:c + PROJ_CHUNK], preferred_element_type=F32).astype(BF16)

    def mix(p_ref, y_ref, rows):
        v = _layer_norm(_gelu_tanh(p_ref[:, GMLP_WIDTH:2 * GMLP_WIDTH].astype(F32)), g_ref[...], b_ref[...])
        v = v.astype(BF16)
        for c in range(tile // CHUNK):
            r = slice(c * CHUNK, (c + 1) * CHUNK)
            for g in range(GMLP_GROUPS):
                cols = slice(g * GMLP_GROUP_DIM, (g + 1) * GMLP_GROUP_DIM)
                zcols = slice(2 * GMLP_WIDTH + g * GMLP_GROUP_DIM, 2 * GMLP_WIDTH + (g + 1) * GMLP_GROUP_DIM)
                mixed = jnp.dot(ws_ref[g], v[r, cols], preferred_element_type=F32) + bias_ref[:, cols]
                u = _gelu_tanh(p_ref[r, cols].astype(F32))
                y_ref[r, cols] = (u * mixed * _silu(p_ref[r, zcols].astype(F32))).astype(BF16)
        gate = _sigmoid(p_ref[:, 3 * GMLP_WIDTH:].astype(F32) + gb_ref[...])
        o_ref[rows, :] = (gate * jnp.dot(y_ref[...], wb_ref[...], preferred_element_type=F32)).astype(BF16)

    lo, hi = slice(0, tile), slice(tile, 2 * tile)
    project(lo, p0_ref)
    mix(p1_ref, y1_ref, lo)
    project(hi, p1_ref)
    mix(p0_ref, y0_ref, hi)


def _gmlp_branch(h_bf16, w_proj, w_branch, ln_g, ln_b, spatial_w, spatial_b, gate_bias):
    s, d = h_bf16.shape
    tile = BRANCH_TILE
    n_pairs = s // (2 * tile)
    bias = jnp.repeat(spatial_b.T, GMLP_GROUP_DIM, axis=1)

    def resident(shape):
        return pl.BlockSpec(shape, lambda i: (0,) * len(shape), pipeline_mode=pl.Buffered(1))

    return pl.pallas_call(
        _gmlp_branch_kernel,
        grid=(n_pairs + 1,),
        in_specs=[pl.BlockSpec((2 * tile, d), lambda i: (jnp.minimum(i, n_pairs - 1), 0)),
                  resident((d, GMLP_PROJ_WIDTH)), resident((GMLP_WIDTH, d)),
                  resident((1, GMLP_WIDTH)), resident((1, GMLP_WIDTH)),
                  resident((GMLP_GROUPS, CHUNK, CHUNK)), resident((CHUNK, GMLP_WIDTH)), resident((1, d))],
        out_specs=pl.BlockSpec((2 * tile, d), lambda i: (i, 0)),
        out_shape=jax.ShapeDtypeStruct(((n_pairs + 1) * 2 * tile, d), BF16),
        scratch_shapes=[pltpu.VMEM((tile, GMLP_PROJ_WIDTH), BF16), pltpu.VMEM((tile, GMLP_PROJ_WIDTH), BF16),
                        pltpu.VMEM((tile, GMLP_WIDTH), BF16), pltpu.VMEM((tile, GMLP_WIDTH), BF16)],
        compiler_params=_params("arbitrary"),
        name="gmlp_branch",
    )(h_bf16, w_proj, w_branch, ln_g.reshape(1, -1), ln_b.reshape(1, -1), spatial_w.astype(BF16), bias,
      gate_bias.reshape(1, d))


def _rope(t, cos, sin):
    lane = lax.broadcasted_iota(jnp.int32, t.shape, 1)
    partner = jnp.where(lane < ROPE_HALF, pltpu.roll(t, HEAD_DIM - ROPE_HALF, 1), pltpu.roll(t, ROPE_HALF, 1))
    return t * cos + partner * sin


def _attn_kernel(sink_ref, qlo_ref, qhi_ref, k_ref, v_ref, z_ref, kp_ref, vp_ref, kn_ref, vn_ref,
                 cos_ref, sin_ref, cosp_ref, sinp_ref, cosn_ref, sinn_ref, o_ref, *, rows):
    i = pl.program_id(0)
    last = pl.num_programs(0) - 1
    n_blocks = rows // BLOCK
    cos_all = jnp.concatenate([cosp_ref[...], cos_ref[...], cosn_ref[...]], axis=0)
    sin_all = jnp.concatenate([sinp_ref[...], sin_ref[...], sinn_ref[...]], axis=0)
    k_all = jnp.concatenate([kp_ref[...], k_ref[...], kn_ref[...]], axis=0).astype(F32)
    v_all = jnp.concatenate([vp_ref[...], v_ref[...], vn_ref[...]], axis=0)
    cos_q = cos_ref[...]
    sin_q = sin_ref[...]
    scale = HEAD_DIM ** -0.5

    q_row = lax.broadcasted_iota(jnp.int32, (GROUP * BLOCK, 3 * BLOCK), 0) % BLOCK
    k_col = lax.broadcasted_iota(jnp.int32, (GROUP * BLOCK, 3 * BLOCK), 1)
    band = (k_col >= q_row) & (k_col <= q_row + 2 * BLOCK)
    head_of_row = lax.broadcasted_iota(jnp.int32, (GROUP * BLOCK, 1), 0) // BLOCK

    for g in range(N_KV_HEADS):
        kv_cols = slice(g * HEAD_DIM, (g + 1) * HEAD_DIM)
        k_rot = _rope(k_all[:, kv_cols], cos_all, sin_all).astype(BF16)
        v_g = v_all[:, kv_cols]
        sink_col = jnp.zeros((GROUP * BLOCK, 1), F32)
        for j in range(GROUP):
            sink_col = jnp.where(head_of_row == j, sink_ref[g * GROUP + j], sink_col)
        q_heads = []
        for j in range(GROUP):
            head = g * GROUP + j
            q_ref = qlo_ref if head < N_Q_HEADS // 2 else qhi_ref
            c0 = (head % (N_Q_HEADS // 2)) * HEAD_DIM
            q = q_ref[:, c0:c0 + HEAD_DIM].astype(F32)
            q_heads.append((_rope(q, cos_q, sin_q) * scale).astype(BF16))
        for b in range(n_blocks):
            r = slice(b * BLOCK, (b + 1) * BLOCK)
            kb = k_rot[b * BLOCK:(b + 3) * BLOCK]
            vb = v_g[b * BLOCK:(b + 3) * BLOCK]
            q4 = jnp.concatenate([qh[r] for qh in q_heads], axis=0)
            scores = lax.dot_general(q4, kb, (((1,), (1,)), ((), ())), preferred_element_type=F32)
            valid = band
            if b == 0:
                valid = valid & (k_col >= jnp.where(i == 0, BLOCK, 0))
            if b == n_blocks - 1:
                valid = valid & (k_col < jnp.where(i == last, 2 * BLOCK, 3 * BLOCK))
            scores = jnp.where(valid, scores, -jnp.inf)
            m = jnp.maximum(jnp.max(scores, axis=-1, keepdims=True), sink_col)
            p = jnp.exp(scores - m)
            denom = jnp.sum(p, axis=-1, keepdims=True) + jnp.exp(sink_col - m)
            out = jnp.dot(p.astype(BF16), vb, preferred_element_type=F32) / denom
            for j in range(GROUP):
                c0 = (g * GROUP + j) * HEAD_DIM
                zj = z_ref[r, c0:c0 + HEAD_DIM].astype(F32)
                o_ref[r, c0:c0 + HEAD_DIM] = (out[j * BLOCK:(j + 1) * BLOCK] * _silu(zj)).astype(o_ref.dtype)


def _attention_mixer(p, sink, cos_tab, sin_tab, rows=256):
    s = p.shape[0]
    blocks = rows // BLOCK
    n_blocks = s // BLOCK

    def own(width, col):
        return pl.BlockSpec((rows, width), lambda i: (i, col))

    def prev(width, col):
        return pl.BlockSpec((BLOCK, width), lambda i: (jnp.maximum(i * blocks - 1, 0), col))

    def nxt(width, col):
        return pl.BlockSpec((BLOCK, width), lambda i: (jnp.minimum((i + 1) * blocks, n_blocks - 1), col))

    half_q = ATTN_WIDTH // 2
    return pl.pallas_call(
        functools.partial(_attn_kernel, rows=rows),
        grid=(s // rows,),
        in_specs=[pl.BlockSpec(memory_space=pltpu.SMEM),
                  own(half_q, COL_Q_LO), own(half_q, COL_Q_HI), own(KV_WIDTH, COL_K), own(KV_WIDTH, COL_V),
                  own(ATTN_WIDTH, COL_ATTN_Z),
                  prev(KV_WIDTH, COL_K), prev(KV_WIDTH, COL_V), nxt(KV_WIDTH, COL_K), nxt(KV_WIDTH, COL_V),
                  own(HEAD_DIM, 0), own(HEAD_DIM, 0), prev(HEAD_DIM, 0), prev(HEAD_DIM, 0),
                  nxt(HEAD_DIM, 0), nxt(HEAD_DIM, 0)],
        out_specs=pl.BlockSpec((rows, ATTN_WIDTH), lambda i: (i, 0)),
        out_shape=jax.ShapeDtypeStruct((s, ATTN_WIDTH), BF16),
        compiler_params=_params("parallel"),
        name="attention_mixer",
    )(sink, p, p, p, p, p, p, p, p, p, cos_tab, sin_tab, cos_tab, sin_tab, cos_tab, sin_tab)


def _merge_kernel(ya_ref, yc_ref, cb_ref, ra_ref, rc_ref, h_ref, wa_ref, wc_ref, wo_ref,
                  gb_ref, g_ref, b_ref, of_ref, ob_ref):
    gb = gb_ref[...]
    merged = _sigmoid(ra_ref[...].astype(F32) + gb[0:1, :]) * jnp.dot(
        ya_ref[...], wa_ref[...], preferred_element_type=F32)
    merged += cb_ref[...].astype(F32)
    merged += _sigmoid(rc_ref[...].astype(F32) + gb[2:3, :]) * jnp.dot(
        yc_ref[...], wc_ref[...], preferred_element_type=F32)
    out = jnp.dot(merged.astype(BF16), wo_ref[...], preferred_element_type=F32)
    y = _layer_norm(ALPHA * h_ref[...] + out, g_ref[...], b_ref[...])
    of_ref[...] = y
    ob_ref[...] = y.astype(BF16)


def _merge_project_norm(y_a, y_c, contrib_b, p, h, w_a, w_c, w_o, gate_b, ln_g, ln_b):
    s, d = h.shape
    rows = BRANCH_TILE

    def row(width, col=0):
        return pl.BlockSpec((rows, width), lambda i: (i, col))

    def resident(shape):
        return pl.BlockSpec(shape, lambda i: (0, 0), pipeline_mode=pl.Buffered(1))

    return pl.pallas_call(
        _merge_kernel,
        grid=(s // rows,),
        in_specs=[row(CONV_WIDTH), row(ATTN_WIDTH), pl.BlockSpec((rows, d), lambda i: (i + 1, 0)),
                  row(d, COL_GATE_A), row(d, COL_GATE_C), row(d),
                  resident((CONV_WIDTH, d)), resident((ATTN_WIDTH, d)),
                  resident((d, d)), resident((3, d)), resident((1, d)), resident((1, d))],
        out_specs=[row(d), row(d)],
        out_shape=[jax.ShapeDtypeStruct((s, d), F32), jax.ShapeDtypeStruct((s, d), BF16)],
        compiler_params=_params("parallel"),
        name="merge_project_norm",
    )(y_a, y_c, contrib_b, p, p, h, w_a, w_c, w_o, gate_b, ln_g.reshape(1, d), ln_b.reshape(1, d))


def kernel(x, positions, ln0_g, ln0_b, w_in, conv_w, gmlp_ln_g, gmlp_ln_b, spatial_w, spatial_b, sink,
           w_branch_a, w_branch_b, w_branch_c, gate_b, w_out, ln_g, ln_b):
    bsz, s, d = x.shape
    assert (bsz, d) == (1, D_MODEL) and s % 1024 == 0 and w_in.shape == (DEPTH, D_MODEL, IN_WIDTH)
    h, h_bf16 = _input_layer_norm(x.reshape(s, d), ln0_g, ln0_b)
    cos_tab, sin_tab = _rope_tables(positions.reshape(s))
    def cols(w, off, width):
        return w[:, off:off + width]

    for l in range(DEPTH):
        w = w_in[l]
        w_main = jnp.concatenate(
            [cols(w, OFF_CONV, 4 * CONV_WIDTH), cols(w, OFF_ATTN_Z, ATTN_WIDTH), cols(w, OFF_GATE_A, d),
             cols(w, OFF_GATE_C, d), cols(w, OFF_Q, ATTN_WIDTH), cols(w, OFF_KV, 2 * KV_WIDTH)],
            axis=1).astype(BF16)
        w_gmlp = jnp.concatenate([cols(w, OFF_GMLP, 3 * GMLP_WIDTH), cols(w, OFF_GATE_B, d)], axis=1).astype(BF16)
        contrib_b = _gmlp_branch(h_bf16, w_gmlp, w_branch_b[l].astype(BF16), gmlp_ln_g[l], gmlp_ln_b[l],
                                 spatial_w[l], spatial_b[l], gate_b[l, 1])
        p = _input_projection(h_bf16, w_main)
        y_a = _conv_mixer(p, conv_w[l])
        y_c = _attention_mixer(p, sink[l], cos_tab, sin_tab)
        h, h_bf16 = _merge_project_norm(
            y_a, y_c, contrib_b, p, h, w_branch_a[l].astype(BF16), w_branch_c[l].astype(BF16),
            w_out[l].astype(BF16), gate_b[l], ln_g[l], ln_b[l])
    return h.reshape(bsz, s, d)
```

```python
import functools

import jax
import jax.numpy as jnp
from jax import lax
from jax.experimental import pallas as pl
from jax.experimental.pallas import tpu as pltpu

D_MODEL = 2048
DEPTH = 2
HEAD_DIM = 128
N_Q_HEADS = 16
N_KV_HEADS = 4
GROUP = N_Q_HEADS // N_KV_HEADS
ATTN_WIDTH = N_Q_HEADS * HEAD_DIM
KV_WIDTH = N_KV_HEADS * HEAD_DIM
CONV_WIDTH = 1024
GMLP_WIDTH = 1024
GMLP_GROUPS = 8
GMLP_GROUP_DIM = GMLP_WIDTH // GMLP_GROUPS
CHUNK = 128
BLOCK = 128
ROPE_THETA = 500000.0
ROPE_DIM = HEAD_DIM // 4
ROPE_HALF = ROPE_DIM // 2
LN_EPS = 1e-5
ALPHA = (2.0 * DEPTH) ** 0.25
IN_WIDTH = 4 * CONV_WIDTH + 3 * GMLP_WIDTH + 2 * ATTN_WIDTH + 2 * KV_WIDTH + 3 * D_MODEL

OFF_CONV, OFF_GMLP, OFF_Q, OFF_KV, OFF_ATTN_Z = 0, 4096, 7168, 9216, 10240
OFF_GATE_A, OFF_GATE_B, OFF_GATE_C = 12288, 14336, 16384

WEIGHT_TILE = 1024
PROJ_CHUNK = 1024
BRANCH_TILE = 256
VMEM_LIMIT_BYTES = 56 * 1024 * 1024

F32 = jnp.float32
BF16 = jnp.bfloat16


def _params(*semantics):
    return pltpu.CompilerParams(dimension_semantics=semantics, vmem_limit_bytes=VMEM_LIMIT_BYTES)


def _silu(x):
    return x / (1.0 + jnp.exp(-x))


def _sigmoid(x):
    return 1.0 / (1.0 + jnp.exp(-x))


def _gelu_tanh(x):
    return 0.5 * x * (1.0 + jnp.tanh(0.7978845608028654 * (x + 0.044715 * (x * x * x))))


def _layer_norm(x, g, b):
    mu = jnp.mean(x, axis=-1, keepdims=True)
    xc = x - mu
    var = jnp.mean(xc * xc, axis=-1, keepdims=True)
    return xc * lax.rsqrt(var + LN_EPS) * g + b


def _resident(shape):
    return pl.BlockSpec(shape, lambda i: (0,) * len(shape), pipeline_mode=pl.Buffered(1))


def _layer_resident(layer, shape):
    return pl.BlockSpec((None,) + shape, lambda i: (layer,) + (0,) * len(shape), pipeline_mode=pl.Buffered(1))


def _ln_kernel(x_ref, g_ref, b_ref, of_ref, ob_ref):
    y = _layer_norm(x_ref[...], g_ref[...], b_ref[...])
    of_ref[...] = y
    ob_ref[...] = y.astype(BF16)


def _input_layer_norm(x, g, b, rows=512):
    s, d = x.shape
    row_spec = pl.BlockSpec((rows, d), lambda i: (i, 0))
    vec_spec = pl.BlockSpec((1, d), lambda i: (0, 0))
    return pl.pallas_call(
        _ln_kernel,
        grid=(s // rows,),
        in_specs=[row_spec, vec_spec, vec_spec],
        out_specs=[row_spec, row_spec],
        out_shape=[jax.ShapeDtypeStruct((s, d), F32), jax.ShapeDtypeStruct((s, d), BF16)],
        compiler_params=_params("parallel"),
        name="input_layer_norm",
    )(x, g.reshape(1, d), b.reshape(1, d))


def _rope_table_kernel(pos_ref, invf_ref, cos_ref, sin_ref):
    ang = pos_ref[...].astype(F32) * invf_ref[...]
    lane = lax.broadcasted_iota(jnp.int32, ang.shape, 1)
    c = jnp.cos(ang)
    s = jnp.sin(ang)
    cos_ref[...] = jnp.where(lane < ROPE_DIM, c, 1.0)
    sin_ref[...] = jnp.where(lane < ROPE_HALF, -s, jnp.where(lane < ROPE_DIM, s, 0.0))


def _rope_tables(positions, rows=512):
    s = positions.shape[0]
    inv_freq = ROPE_THETA ** (-jnp.arange(ROPE_HALF, dtype=F32) / ROPE_HALF)
    invf = jnp.tile(inv_freq, HEAD_DIM // ROPE_HALF).reshape(1, HEAD_DIM)
    tab = jax.ShapeDtypeStruct((s, HEAD_DIM), F32)
    tab_spec = pl.BlockSpec((rows, HEAD_DIM), lambda i: (i, 0))
    return pl.pallas_call(
        _rope_table_kernel,
        grid=(s // rows,),
        in_specs=[pl.BlockSpec((rows, 1), lambda i: (i, 0)), pl.BlockSpec((1, HEAD_DIM), lambda i: (0, 0))],
        out_specs=[tab_spec, tab_spec],
        out_shape=[tab, tab],
        compiler_params=_params("parallel"),
        name="rope_tables",
    )(positions.reshape(s, 1), invf)


def _rope(t, cos, sin):
    lane = lax.broadcasted_iota(jnp.int32, t.shape, 1)
    partner = jnp.where(lane < ROPE_HALF, pltpu.roll(t, HEAD_DIM - ROPE_HALF, 1), pltpu.roll(t, ROPE_HALF, 1))
    return t * cos + partner * sin


def _cast_kernel(tiles_ref, w_ref, o_ref):
    del tiles_ref
    o_ref[...] = w_ref[...].astype(o_ref.dtype)


def _gather_cast_columns(w, col_tiles, tile=WEIGHT_TILE):
    layers, k, _ = w.shape
    table = jnp.asarray(col_tiles, jnp.int32)
    return pl.pallas_call(
        _cast_kernel,
        grid_spec=pltpu.PrefetchScalarGridSpec(
            num_scalar_prefetch=1,
            grid=(layers, len(col_tiles)),
            in_specs=[pl.BlockSpec((None, k, tile), lambda l, j, t: (l, 0, t[j]))],
            out_specs=pl.BlockSpec((None, k, tile), lambda l, j, t: (l, 0, j)),
        ),
        out_shape=jax.ShapeDtypeStruct((layers, k, len(col_tiles) * tile), BF16),
        compiler_params=_params("parallel", "parallel"),
        name="gather_cast_columns",
    )(table, w)


def _cast_bf16(w):
    return _gather_cast_columns(w, tuple(range(w.shape[-1] // WEIGHT_TILE)))


def _project(h_ref, rows, w_ref, p_ref):
    h = h_ref[rows, :]
    for c in range(0, p_ref.shape[1], PROJ_CHUNK):
        p_ref[:, c:c + PROJ_CHUNK] = jnp.dot(
            h, w_ref[:, c:c + PROJ_CHUNK], preferred_element_type=F32).astype(BF16)


def _branch_call(body, name, h_bf16, w_proj, w_branch, layer, extra_inputs, extra_specs, y_width, out_width):
    s, d = h_bf16.shape
    tile = BRANCH_TILE
    n_pairs = s // (2 * tile)
    proj_width = w_proj.shape[2]
    return pl.pallas_call(
        body,
        grid=(n_pairs + 1,),
        in_specs=[pl.BlockSpec((2 * tile, d), lambda i: (jnp.minimum(i, n_pairs - 1), 0)),
                  _layer_resident(layer, (d, proj_width)), _layer_resident(layer, w_branch.shape[1:])] + extra_specs,
        out_specs=pl.BlockSpec((2 * tile, out_width), lambda i: (i, 0)),
        out_shape=jax.ShapeDtypeStruct(((n_pairs + 1) * 2 * tile, out_width), BF16),
        scratch_shapes=[pltpu.VMEM((tile, proj_width), BF16), pltpu.VMEM((tile, proj_width), BF16),
                        pltpu.VMEM((tile, y_width), BF16), pltpu.VMEM((tile, y_width), BF16)],
        compiler_params=_params("arbitrary"),
        name=name,
    )(h_bf16, w_proj, w_branch, *extra_inputs)


def _conv_branch_kernel(h_ref, w_ref, wa_ref, cw_ref, gb_ref, o_ref, p0_ref, p1_ref, y0_ref, y1_ref):
    tile = BRANCH_TILE
    s = pl.program_id(0)
    n_tiles = 2 * (pl.num_programs(0) - 1)
    cw = cw_ref[...]

    @pl.when(s == 0)
    def _():
        p0_ref[...] = jnp.zeros_like(p0_ref)
        p1_ref[...] = jnp.zeros_like(p1_ref)

    def gated_input_row(p_ref, row):
        c = p_ref[row:row + 1, CONV_WIDTH:2 * CONV_WIDTH].astype(F32)
        return c * p_ref[row:row + 1, 2 * CONV_WIDTH:3 * CONV_WIDTH].astype(F32)

    def mix(p_ref, y_ref, rows, tile_idx, y_prev, y_next):
        y = p_ref[:, CONV_WIDTH:2 * CONV_WIDTH].astype(F32) * p_ref[:, 2 * CONV_WIDTH:3 * CONV_WIDTH].astype(F32)
        y_prev = jnp.where(tile_idx > 0, y_prev, 0.0)
        y_next = jnp.where(tile_idx < n_tiles - 1, y_next, 0.0)
        row = lax.broadcasted_iota(jnp.int32, y.shape, 0)
        up = jnp.where(row == 0, y_prev, pltpu.roll(y, 1, 0))
        dn = jnp.where(row == tile - 1, y_next, pltpu.roll(y, tile - 1, 0))
        conv = cw[0:1, :] * up + cw[1:2, :] * y + cw[2:3, :] * dn
        z = p_ref[:, 3 * CONV_WIDTH:4 * CONV_WIDTH].astype(F32)
        y_ref[...] = (p_ref[:, :CONV_WIDTH].astype(F32) * conv * _silu(z)).astype(BF16)
        gate = _sigmoid(p_ref[:, 4 * CONV_WIDTH:].astype(F32) + gb_ref[...])
        o_ref[rows, :] = (gate * jnp.dot(y_ref[...], wa_ref[...], preferred_element_type=F32)).astype(BF16)

    lo, hi = slice(0, tile), slice(tile, 2 * tile)
    last_of_tile_before_p1 = gated_input_row(p0_ref, tile - 1)
    _project(h_ref, lo, w_ref, p0_ref)
    last_of_p1 = gated_input_row(p1_ref, tile - 1)
    mix(p1_ref, y1_ref, lo, 2 * s - 1, last_of_tile_before_p1, gated_input_row(p0_ref, 0))
    _project(h_ref, hi, w_ref, p1_ref)
    mix(p0_ref, y0_ref, hi, 2 * s, last_of_p1, gated_input_row(p1_ref, 0))


def _conv_branch(h_bf16, w_proj, w_branch, layer, conv_w, gate_bias):
    d = h_bf16.shape[1]
    return _branch_call(_conv_branch_kernel, "conv_branch", h_bf16, w_proj, w_branch, layer,
                        [conv_w, gate_bias.reshape(1, d)], [_resident((3, CONV_WIDTH)), _resident((1, d))],
                        CONV_WIDTH, d)


def _gmlp_branch_kernel(h_ref, w_ref, wb_ref, g_ref, b_ref, ws_ref, bias_ref, gb_ref, o_ref,
                        p0_ref, p1_ref, y0_ref, y1_ref):
    tile = BRANCH_TILE

    @pl.when(pl.program_id(0) == 0)
    def _():
        p1_ref[...] = jnp.zeros_like(p1_ref)

    def mix(p_ref, y_ref, rows):
        v = _layer_norm(_gelu_tanh(p_ref[:, GMLP_WIDTH:2 * GMLP_WIDTH].astype(F32)), g_ref[...], b_ref[...])
        v = v.astype(BF16)
        for c in range(tile // CHUNK):
            r = slice(c * CHUNK, (c + 1) * CHUNK)
            for g in range(GMLP_GROUPS):
                cols = slice(g * GMLP_GROUP_DIM, (g + 1) * GMLP_GROUP_DIM)
                zcols = slice(2 * GMLP_WIDTH + g * GMLP_GROUP_DIM, 2 * GMLP_WIDTH + (g + 1) * GMLP_GROUP_DIM)
                mixed = jnp.dot(ws_ref[g], v[r, cols], preferred_element_type=F32) + bias_ref[:, cols]
                u = _gelu_tanh(p_ref[r, cols].astype(F32))
                y_ref[r, cols] = (u * mixed * _silu(p_ref[r, zcols].astype(F32))).astype(BF16)
        gate = _sigmoid(p_ref[:, 3 * GMLP_WIDTH:].astype(F32) + gb_ref[...])
        o_ref[rows, :] = (gate * jnp.dot(y_ref[...], wb_ref[...], preferred_element_type=F32)).astype(BF16)

    lo, hi = slice(0, tile), slice(tile, 2 * tile)
    _project(h_ref, lo, w_ref, p0_ref)
    mix(p1_ref, y1_ref, lo)
    _project(h_ref, hi, w_ref, p1_ref)
    mix(p0_ref, y0_ref, hi)


def _gmlp_branch(h_bf16, w_proj, w_branch, layer, ln_g, ln_b, spatial_w, spatial_b, gate_bias):
    d = h_bf16.shape[1]
    bias = jnp.repeat(spatial_b.T, GMLP_GROUP_DIM, axis=1)
    return _branch_call(
        _gmlp_branch_kernel, "gmlp_branch", h_bf16, w_proj, w_branch, layer,
        [ln_g.reshape(1, -1), ln_b.reshape(1, -1), spatial_w.astype(BF16), bias, gate_bias.reshape(1, d)],
        [_resident((1, GMLP_WIDTH)), _resident((1, GMLP_WIDTH)), _resident((GMLP_GROUPS, CHUNK, CHUNK)),
         _resident((CHUNK, GMLP_WIDTH)), _resident((1, d))],
        GMLP_WIDTH, d)


def _kv_kernel(h_ref, w_ref, cos_ref, sin_ref, o_ref):
    i = pl.program_id(0)
    kv = jnp.dot(h_ref[...], w_ref[...], preferred_element_type=F32)
    cos = cos_ref[...]
    sin = sin_ref[...]
    parts = [_rope(kv[:, g * HEAD_DIM:(g + 1) * HEAD_DIM], cos, sin) for g in range(N_KV_HEADS)]
    out = jnp.concatenate(parts + [kv[:, KV_WIDTH:]], axis=1)
    inside = (i > 0) & (i < pl.num_programs(0) - 1)
    o_ref[...] = jnp.where(inside, out, 0.0).astype(BF16)


def _kv_projection(h_bf16, w_kv, layer, cos_pad, sin_pad):
    s, d = h_bf16.shape
    tile = BRANCH_TILE
    n_tiles = s // tile
    tab_spec = pl.BlockSpec((tile, HEAD_DIM), lambda i: (i, 0))
    return pl.pallas_call(
        _kv_kernel,
        grid=(n_tiles + 2,),
        in_specs=[pl.BlockSpec((tile, d), lambda i: (jnp.clip(i - 1, 0, n_tiles - 1), 0)),
                  _layer_resident(layer, (d, 2 * KV_WIDTH)), tab_spec, tab_spec],
        out_specs=pl.BlockSpec((tile, 2 * KV_WIDTH), lambda i: (i, 0)),
        out_shape=jax.ShapeDtypeStruct(((n_tiles + 2) * tile, 2 * KV_WIDTH), BF16),
        compiler_params=_params("parallel"),
        name="kv_projection",
    )(h_bf16, w_kv, cos_pad, sin_pad)


def _attn_branch_kernel(sink_ref, h_ref, w_ref, wc_ref, kv_ref, kvp_ref, kvn_ref, cos_ref, sin_ref, o_ref,
                        p0_ref, p1_ref, y0_ref, y1_ref):
    tile = BRANCH_TILE
    n_blocks = tile // BLOCK
    s = pl.program_id(0)
    n_tiles = 2 * (pl.num_programs(0) - 1)
    scale = HEAD_DIM ** -0.5

    @pl.when(s == 0)
    def _():
        p1_ref[...] = jnp.zeros_like(p1_ref)

    q_row = lax.broadcasted_iota(jnp.int32, (GROUP * BLOCK, 3 * BLOCK), 0) % BLOCK
    k_col = lax.broadcasted_iota(jnp.int32, (GROUP * BLOCK, 3 * BLOCK), 1)
    band = (k_col >= q_row) & (k_col <= q_row + 2 * BLOCK)
    head_of_row = lax.broadcasted_iota(jnp.int32, (GROUP * BLOCK, 1), 0) // BLOCK

    def band_rows(w, b, cols):
        first = w * n_blocks + b
        if first == 0:
            return jnp.concatenate([kvp_ref[:, cols], kv_ref[:2 * BLOCK, cols]], axis=0)
        if first == 2 * n_blocks - 1:
            return jnp.concatenate([kv_ref[(first - 1) * BLOCK:, cols], kvn_ref[:, cols]], axis=0)
        return kv_ref[(first - 1) * BLOCK:(first + 2) * BLOCK, cols]

    def attend(p_ref, y_ref, w, tile_idx):
        rows = slice(w * tile, (w + 1) * tile)
        cos_q = cos_ref[rows, :]
        sin_q = sin_ref[rows, :]
        for g in range(N_KV_HEADS):
            sink_col = jnp.zeros((GROUP * BLOCK, 1), F32)
            for j in range(GROUP):
                sink_col = jnp.where(head_of_row == j, sink_ref[g * GROUP + j], sink_col)
            q_heads = []
            for j in range(GROUP):
                c0 = (g * GROUP + j) * HEAD_DIM
                q = p_ref[:, c0:c0 + HEAD_DIM].astype(F32)
                q_heads.append((_rope(q, cos_q, sin_q) * scale).astype(BF16))
            for b in range(n_blocks):
                r = slice(b * BLOCK, (b + 1) * BLOCK)
                kb = band_rows(w, b, slice(g * HEAD_DIM, (g + 1) * HEAD_DIM))
                vb = band_rows(w, b, slice(KV_WIDTH + g * HEAD_DIM, KV_WIDTH + (g + 1) * HEAD_DIM))
                q4 = jnp.concatenate([qh[r] for qh in q_heads], axis=0)
                scores = lax.dot_general(q4, kb, (((1,), (1,)), ((), ())), preferred_element_type=F32)
                valid = band
                if b == 0:
                    valid = valid & (k_col >= jnp.where(tile_idx == 0, BLOCK, 0))
                if b == n_blocks - 1:
                    valid = valid & (k_col < jnp.where(tile_idx == n_tiles - 1, 2 * BLOCK, 3 * BLOCK))
                scores = jnp.where(valid, scores, -jnp.inf)
                m = jnp.maximum(jnp.max(scores, axis=-1, keepdims=True), sink_col)
                p = jnp.exp(scores - m)
                denom = jnp.sum(p, axis=-1, keepdims=True) + jnp.exp(sink_col - m)
                out = jnp.dot(p.astype(BF16), vb, preferred_element_type=F32) / denom
                for j in range(GROUP):
                    c0 = (g * GROUP + j) * HEAD_DIM
                    zj = p_ref[r, ATTN_WIDTH + c0:ATTN_WIDTH + c0 + HEAD_DIM].astype(F32)
                    y_ref[r, c0:c0 + HEAD_DIM] = (out[j * BLOCK:(j + 1) * BLOCK] * _silu(zj)).astype(BF16)
        o_ref[rows, :] = jnp.dot(y_ref[...], wc_ref[...], preferred_element_type=F32).astype(BF16)

    lo, hi = slice(0, tile), slice(tile, 2 * tile)
    _project(h_ref, lo, w_ref, p0_ref)
    attend(p1_ref, y1_ref, 0, 2 * s - 1)
    _project(h_ref, hi, w_ref, p1_ref)
    attend(p0_ref, y0_ref, 1, 2 * s)


def _attention_branch(h_bf16, w_proj, w_branch, layer, sink, kv_pad, cos_pad, sin_pad):
    d = h_bf16.shape[1]
    tile = BRANCH_TILE
    halo_per_pair = 2 * tile // BLOCK
    n_halo = kv_pad.shape[0] // BLOCK
    kv_width = 2 * KV_WIDTH
    tab_spec = pl.BlockSpec((2 * tile, HEAD_DIM), lambda i: (i, 0))
    specs = [pl.BlockSpec((2 * tile, kv_width), lambda i: (i, 0)),
             pl.BlockSpec((BLOCK, kv_width), lambda i: (jnp.maximum(i * halo_per_pair - 1, 0), 0)),
             pl.BlockSpec((BLOCK, kv_width), lambda i: (jnp.minimum((i + 1) * halo_per_pair, n_halo - 1), 0)),
             tab_spec, tab_spec]
    s, _ = h_bf16.shape
    n_pairs = s // (2 * tile)
    return pl.pallas_call(
        _attn_branch_kernel,
        grid=(n_pairs + 1,),
        in_specs=[pl.BlockSpec(memory_space=pltpu.SMEM),
                  pl.BlockSpec((2 * tile, d), lambda i: (jnp.minimum(i, n_pairs - 1), 0)),
                  _layer_resident(layer, (d, 2 * ATTN_WIDTH)), _layer_resident(layer, (ATTN_WIDTH, d))] + specs,
        out_specs=pl.BlockSpec((2 * tile, d), lambda i: (i, 0)),
        out_shape=jax.ShapeDtypeStruct(((n_pairs + 1) * 2 * tile, d), BF16),
        scratch_shapes=[pltpu.VMEM((tile, 2 * ATTN_WIDTH), BF16), pltpu.VMEM((tile, 2 * ATTN_WIDTH), BF16),
                        pltpu.VMEM((tile, ATTN_WIDTH), BF16), pltpu.VMEM((tile, ATTN_WIDTH), BF16)],
        compiler_params=_params("arbitrary"),
        name="attention_branch",
    )(sink, h_bf16, w_proj, w_branch, kv_pad, kv_pad, kv_pad, cos_pad, sin_pad)


def _final_kernel(ca_ref, cb_ref, cc_ref, hb_ref, h_ref, wr_ref, wo_ref, gb_ref, g_ref, b_ref, *out_refs):
    gate_c = _sigmoid(jnp.dot(hb_ref[...], wr_ref[...], preferred_element_type=F32) + gb_ref[...])
    merged = ca_ref[...].astype(F32) + cb_ref[...].astype(F32) + gate_c * cc_ref[...].astype(F32)
    out = jnp.dot(merged.astype(BF16), wo_ref[...], preferred_element_type=F32)
    y = _layer_norm(ALPHA * h_ref[...] + out, g_ref[...], b_ref[...])
    out_refs[0][...] = y
    if len(out_refs) > 1:
        out_refs[1][...] = y.astype(BF16)


def _merge_project_norm(contrib_a, contrib_b, contrib_c, h_bf16, h, w_gate_c, w_o, layer, gate_bias_c, ln_g, ln_b,
                        emit_bf16):
    s, d = h.shape
    rows = BRANCH_TILE
    row = pl.BlockSpec((rows, d), lambda i: (i, 0))
    shifted = pl.BlockSpec((rows, d), lambda i: (i + 1, 0))
    out_specs = [row, row] if emit_bf16 else [row]
    out_shape = [jax.ShapeDtypeStruct((s, d), F32)] + ([jax.ShapeDtypeStruct((s, d), BF16)] if emit_bf16 else [])
    return pl.pallas_call(
        _final_kernel,
        grid=(s // rows,),
        in_specs=[shifted, shifted, shifted, row, row,
                  _layer_resident(layer, (d, d)), _layer_resident(layer, (d, d)),
                  _resident((1, d)), _resident((1, d)), _resident((1, d))],
        out_specs=out_specs,
        out_shape=out_shape,
        compiler_params=_params("parallel"),
        name="merge_project_norm",
    )(contrib_a, contrib_b, contrib_c, h_bf16, h, w_gate_c, w_o, gate_bias_c.reshape(1, d), ln_g.reshape(1, d),
      ln_b.reshape(1, d))


def kernel(x, positions, ln0_g, ln0_b, w_in, conv_w, gmlp_ln_g, gmlp_ln_b, spatial_w, spatial_b, sink,
           w_branch_a, w_branch_b, w_branch_c, gate_b, w_out, ln_g, ln_b):
    bsz, s, d = x.shape
    assert (bsz, d) == (1, D_MODEL) and s % 1024 == 0 and w_in.shape == (DEPTH, D_MODEL, IN_WIDTH)

    def tiles(off, width):
        return tuple(range(off // WEIGHT_TILE, (off + width) // WEIGHT_TILE))

    w_conv = _gather_cast_columns(w_in, tiles(OFF_CONV, 4 * CONV_WIDTH) + tiles(OFF_GATE_A, d))
    w_gmlp = _gather_cast_columns(w_in, tiles(OFF_GMLP, 3 * GMLP_WIDTH) + tiles(OFF_GATE_B, d))
    w_qz = _gather_cast_columns(w_in, tiles(OFF_Q, ATTN_WIDTH) + tiles(OFF_ATTN_Z, ATTN_WIDTH))
    w_kv = _gather_cast_columns(w_in, tiles(OFF_KV, 2 * KV_WIDTH))
    w_gate_c = _gather_cast_columns(w_in, tiles(OFF_GATE_C, d))
    wa, wb, wc, wo = (_cast_bf16(w) for w in (w_branch_a, w_branch_b, w_branch_c, w_out))

    h, h_bf16 = _input_layer_norm(x.reshape(s, d), ln0_g, ln0_b)
    cos_pad, sin_pad = _rope_tables(jnp.pad(positions.reshape(s), (BRANCH_TILE, BRANCH_TILE)))
    for l in range(DEPTH):
        contrib_a = _conv_branch(h_bf16, w_conv, wa, l, conv_w[l], gate_b[l, 0])
        contrib_b = _gmlp_branch(h_bf16, w_gmlp, wb, l, gmlp_ln_g[l], gmlp_ln_b[l],
                                 spatial_w[l], spatial_b[l], gate_b[l, 1])
        kv_pad = _kv_projection(h_bf16, w_kv, l, cos_pad, sin_pad)
        contrib_c = _attention_branch(h_bf16, w_qz, wc, l, sink[l], kv_pad, cos_pad, sin_pad)
        outs = _merge_project_norm(contrib_a, contrib_b, contrib_c, h_bf16, h, w_gate_c, wo, l, gate_b[l, 2],
                                   ln_g[l], ln_b[l], emit_bf16=l + 1 < DEPTH)
        h = outs[0]
        h_bf16 = outs[1] if l + 1 < DEPTH else None
    return h.reshape(bsz, s, d)
```

```python
import functools

import jax
import jax.numpy as jnp
from jax import lax
from jax.experimental import pallas as pl
from jax.experimental.pallas import tpu as pltpu

D_MODEL = 2048
DEPTH = 2
HEAD_DIM = 128
N_Q_HEADS = 16
N_KV_HEADS = 4
GROUP = N_Q_HEADS // N_KV_HEADS
ATTN_WIDTH = N_Q_HEADS * HEAD_DIM
KV_WIDTH = N_KV_HEADS * HEAD_DIM
CONV_WIDTH = 1024
GMLP_WIDTH = 1024
GMLP_GROUPS = 8
GMLP_GROUP_DIM = GMLP_WIDTH // GMLP_GROUPS
CHUNK = 128
BLOCK = 128
ROPE_THETA = 500000.0
ROPE_DIM = HEAD_DIM // 4
ROPE_HALF = ROPE_DIM // 2
LN_EPS = 1e-5
ALPHA = (2.0 * DEPTH) ** 0.25
IN_WIDTH = 4 * CONV_WIDTH + 3 * GMLP_WIDTH + 2 * ATTN_WIDTH + 2 * KV_WIDTH + 3 * D_MODEL

OFF_CONV, OFF_GMLP, OFF_Q, OFF_KV, OFF_ATTN_Z = 0, 4096, 7168, 9216, 10240
OFF_GATE_A, OFF_GATE_B, OFF_GATE_C = 12288, 14336, 16384

WEIGHT_TILE = 1024
PROJ_CHUNK = 1024
BRANCH_TILE = 256
VMEM_LIMIT_BYTES = 56 * 1024 * 1024

F32 = jnp.float32
BF16 = jnp.bfloat16


def _params(*semantics):
    return pltpu.CompilerParams(dimension_semantics=semantics, vmem_limit_bytes=VMEM_LIMIT_BYTES)


def _silu(x):
    return x / (1.0 + jnp.exp(-x))


def _sigmoid(x):
    return 1.0 / (1.0 + jnp.exp(-x))


def _gelu_tanh(x):
    return 0.5 * x * (1.0 + jnp.tanh(0.7978845608028654 * (x + 0.044715 * (x * x * x))))


def _layer_norm(x, g, b):
    mu = jnp.mean(x, axis=-1, keepdims=True)
    xc = x - mu
    var = jnp.mean(xc * xc, axis=-1, keepdims=True)
    return xc * lax.rsqrt(var + LN_EPS) * g + b


def _resident(shape):
    return pl.BlockSpec(shape, lambda i: (0,) * len(shape), pipeline_mode=pl.Buffered(1))


def _layer_resident(layer, shape):
    return pl.BlockSpec((None,) + shape, lambda i: (layer,) + (0,) * len(shape), pipeline_mode=pl.Buffered(1))


def _ln_kernel(x_ref, g_ref, b_ref, of_ref, ob_ref):
    y = _layer_norm(x_ref[...], g_ref[...], b_ref[...])
    of_ref[...] = y
    ob_ref[...] = y.astype(BF16)


def _input_layer_norm(x, g, b, rows=512):
    s, d = x.shape
    row_spec = pl.BlockSpec((rows, d), lambda i: (i, 0))
    vec_spec = pl.BlockSpec((1, d), lambda i: (0, 0))
    return pl.pallas_call(
        _ln_kernel,
        grid=(s // rows,),
        in_specs=[row_spec, vec_spec, vec_spec],
        out_specs=[row_spec, row_spec],
        out_shape=[jax.ShapeDtypeStruct((s, d), F32), jax.ShapeDtypeStruct((s, d), BF16)],
        compiler_params=_params("parallel"),
        name="input_layer_norm",
    )(x, g.reshape(1, d), b.reshape(1, d))


def _rope_table_kernel(pos_ref, invf_ref, cos_ref, sin_ref):
    ang = pos_ref[...].astype(F32) * invf_ref[...]
    lane = lax.broadcasted_iota(jnp.int32, ang.shape, 1)
    c = jnp.cos(ang)
    s = jnp.sin(ang)
    cos_ref[...] = jnp.where(lane < ROPE_DIM, c, 1.0)
    sin_ref[...] = jnp.where(lane < ROPE_HALF, -s, jnp.where(lane < ROPE_DIM, s, 0.0))


def _rope_tables(positions, rows=512):
    s = positions.shape[0]
    inv_freq = ROPE_THETA ** (-jnp.arange(ROPE_HALF, dtype=F32) / ROPE_HALF)
    invf = jnp.tile(inv_freq, HEAD_DIM // ROPE_HALF).reshape(1, HEAD_DIM)
    tab = jax.ShapeDtypeStruct((s, HEAD_DIM), F32)
    tab_spec = pl.BlockSpec((rows, HEAD_DIM), lambda i: (i, 0))
    return pl.pallas_call(
        _rope_table_kernel,
        grid=(s // rows,),
        in_specs=[pl.BlockSpec((rows, 1), lambda i: (i, 0)), pl.BlockSpec((1, HEAD_DIM), lambda i: (0, 0))],
        out_specs=[tab_spec, tab_spec],
        out_shape=[tab, tab],
        compiler_params=_params("parallel"),
        name="rope_tables",
    )(positions.reshape(s, 1), invf)


def _rope(t, cos, sin):
    lane = lax.broadcasted_iota(jnp.int32, t.shape, 1)
    partner = jnp.where(lane < ROPE_HALF, pltpu.roll(t, HEAD_DIM - ROPE_HALF, 1), pltpu.roll(t, ROPE_HALF, 1))
    return t * cos + partner * sin


def _cast_kernel(tiles_ref, w_ref, o_ref):
    del tiles_ref
    o_ref[...] = w_ref[...].astype(o_ref.dtype)


def _gather_cast_columns(w, col_tiles, tile=WEIGHT_TILE):
    layers, k, _ = w.shape
    table = jnp.asarray(col_tiles, jnp.int32)
    return pl.pallas_call(
        _cast_kernel,
        grid_spec=pltpu.PrefetchScalarGridSpec(
            num_scalar_prefetch=1,
            grid=(layers, len(col_tiles)),
            in_specs=[pl.BlockSpec((None, k, tile), lambda l, j, t: (l, 0, t[j]))],
            out_specs=pl.BlockSpec((None, k, tile), lambda l, j, t: (l, 0, j)),
        ),
        out_shape=jax.ShapeDtypeStruct((layers, k, len(col_tiles) * tile), BF16),
        compiler_params=_params("parallel", "parallel"),
        name="gather_cast_columns",
    )(table, w)


def _cast_bf16(w):
    return _gather_cast_columns(w, tuple(range(w.shape[-1] // WEIGHT_TILE)))


def _project(h_ref, rows, w_ref, p_ref):
    h = h_ref[rows, :]
    for c in range(0, p_ref.shape[1], PROJ_CHUNK):
        p_ref[:, c:c + PROJ_CHUNK] = jnp.dot(
            h, w_ref[:, c:c + PROJ_CHUNK], preferred_element_type=F32).astype(BF16)


def _branch_call(body, name, h_bf16, w_proj, w_branch, layer, extra_inputs, extra_specs, y_width, out_width):
    s, d = h_bf16.shape
    tile = BRANCH_TILE
    n_pairs = s // (2 * tile)
    proj_width = w_proj.shape[2]
    return pl.pallas_call(
        body,
        grid=(n_pairs + 1,),
        in_specs=[pl.BlockSpec((2 * tile, d), lambda i: (jnp.minimum(i, n_pairs - 1), 0)),
                  _layer_resident(layer, (d, proj_width)), _layer_resident(layer, w_branch.shape[1:])] + extra_specs,
        out_specs=pl.BlockSpec((2 * tile, out_width), lambda i: (i, 0)),
        out_shape=jax.ShapeDtypeStruct(((n_pairs + 1) * 2 * tile, out_width), BF16),
        scratch_shapes=[pltpu.VMEM((tile, proj_width), BF16), pltpu.VMEM((tile, proj_width), BF16),
                        pltpu.VMEM((tile, y_width), BF16), pltpu.VMEM((tile, y_width), BF16)],
        compiler_params=_params("arbitrary"),
        name=name,
    )(h_bf16, w_proj, w_branch, *extra_inputs)


def _conv_branch_kernel(h_ref, w_ref, wa_ref, cw_ref, gb_ref, o_ref, p0_ref, p1_ref, y0_ref, y1_ref):
    tile = BRANCH_TILE
    s = pl.program_id(0)
    n_tiles = 2 * (pl.num_programs(0) - 1)
    cw = cw_ref[...]

    @pl.when(s == 0)
    def _():
        p0_ref[...] = jnp.zeros_like(p0_ref)
        p1_ref[...] = jnp.zeros_like(p1_ref)

    def gated_input_row(p_ref, row):
        c = p_ref[row:row + 1, CONV_WIDTH:2 * CONV_WIDTH].astype(F32)
        return c * p_ref[row:row + 1, 2 * CONV_WIDTH:3 * CONV_WIDTH].astype(F32)

    def mix(p_ref, y_ref, rows, tile_idx, y_prev, y_next):
        y = p_ref[:, CONV_WIDTH:2 * CONV_WIDTH].astype(F32) * p_ref[:, 2 * CONV_WIDTH:3 * CONV_WIDTH].astype(F32)
        y_prev = jnp.where(tile_idx > 0, y_prev, 0.0)
        y_next = jnp.where(tile_idx < n_tiles - 1, y_next, 0.0)
        row = lax.broadcasted_iota(jnp.int32, y.shape, 0)
        up = jnp.where(row == 0, y_prev, pltpu.roll(y, 1, 0))
        dn = jnp.where(row == tile - 1, y_next, pltpu.roll(y, tile - 1, 0))
        conv = cw[0:1, :] * up + cw[1:2, :] * y + cw[2:3, :] * dn
        z = p_ref[:, 3 * CONV_WIDTH:4 * CONV_WIDTH].astype(F32)
        y_ref[...] = (p_ref[:, :CONV_WIDTH].astype(F32) * conv * _silu(z)).astype(BF16)
        gate = _sigmoid(p_ref[:, 4 * CONV_WIDTH:].astype(F32) + gb_ref[...])
        o_ref[rows, :] = (gate * jnp.dot(y_ref[...], wa_ref[...], preferred_element_type=F32)).astype(BF16)

    lo, hi = slice(0, tile), slice(tile, 2 * tile)
    last_of_tile_before_p1 = gated_input_row(p0_ref, tile - 1)
    _project(h_ref, lo, w_ref, p0_ref)
    last_of_p1 = gated_input_row(p1_ref, tile - 1)
    mix(p1_ref, y1_ref, lo, 2 * s - 1, last_of_tile_before_p1, gated_input_row(p0_ref, 0))
    _project(h_ref, hi, w_ref, p1_ref)
    mix(p0_ref, y0_ref, hi, 2 * s, last_of_p1, gated_input_row(p1_ref, 0))


def _conv_branch(h_bf16, w_proj, w_branch, layer, conv_w, gate_bias):
    d = h_bf16.shape[1]
    return _branch_call(_conv_branch_kernel, "conv_branch", h_bf16, w_proj, w_branch, layer,
                        [conv_w, gate_bias.reshape(1, d)], [_resident((3, CONV_WIDTH)), _resident((1, d))],
                        CONV_WIDTH, d)


def _gmlp_branch_kernel(h_ref, w_ref, wb_ref, g_ref, b_ref, ws_ref, bias_ref, gb_ref, o_ref,
                        p0_ref, p1_ref, y0_ref, y1_ref):
    tile = BRANCH_TILE

    @pl.when(pl.program_id(0) == 0)
    def _():
        p1_ref[...] = jnp.zeros_like(p1_ref)

    def mix(p_ref, y_ref, rows):
        v = _layer_norm(_gelu_tanh(p_ref[:, GMLP_WIDTH:2 * GMLP_WIDTH].astype(F32)), g_ref[...], b_ref[...])
        v = v.astype(BF16)
        for c in range(tile // CHUNK):
            r = slice(c * CHUNK, (c + 1) * CHUNK)
            for g in range(GMLP_GROUPS):
                cols = slice(g * GMLP_GROUP_DIM, (g + 1) * GMLP_GROUP_DIM)
                zcols = slice(2 * GMLP_WIDTH + g * GMLP_GROUP_DIM, 2 * GMLP_WIDTH + (g + 1) * GMLP_GROUP_DIM)
                mixed = jnp.dot(ws_ref[g], v[r, cols], preferred_element_type=F32) + bias_ref[:, cols]
                u = _gelu_tanh(p_ref[r, cols].astype(F32))
                y_ref[r, cols] = (u * mixed * _silu(p_ref[r, zcols].astype(F32))).astype(BF16)
        gate = _sigmoid(p_ref[:, 3 * GMLP_WIDTH:].astype(F32) + gb_ref[...])
        o_ref[rows, :] = (gate * jnp.dot(y_ref[...], wb_ref[...], preferred_element_type=F32)).astype(BF16)

    lo, hi = slice(0, tile), slice(tile, 2 * tile)
    _project(h_ref, lo, w_ref, p0_ref)
    mix(p1_ref, y1_ref, lo)
    _project(h_ref, hi, w_ref, p1_ref)
    mix(p0_ref, y0_ref, hi)


def _gmlp_branch(h_bf16, w_proj, w_branch, layer, ln_g, ln_b, spatial_w, spatial_b, gate_bias):
    d = h_bf16.shape[1]
    bias = jnp.repeat(spatial_b.T, GMLP_GROUP_DIM, axis=1)
    return _branch_call(
        _gmlp_branch_kernel, "gmlp_branch", h_bf16, w_proj, w_branch, layer,
        [ln_g.reshape(1, -1), ln_b.reshape(1, -1), spatial_w.astype(BF16), bias, gate_bias.reshape(1, d)],
        [_resident((1, GMLP_WIDTH)), _resident((1, GMLP_WIDTH)), _resident((GMLP_GROUPS, CHUNK, CHUNK)),
         _resident((CHUNK, GMLP_WIDTH)), _resident((1, d))],
        GMLP_WIDTH, d)


def _kv_kernel(h_ref, w_ref, cos_ref, sin_ref, o_ref):
    i = pl.program_id(0)
    kv = jnp.dot(h_ref[...], w_ref[...], preferred_element_type=F32)
    cos = cos_ref[...]
    sin = sin_ref[...]
    parts = [_rope(kv[:, g * HEAD_DIM:(g + 1) * HEAD_DIM], cos, sin) for g in range(N_KV_HEADS)]
    out = jnp.concatenate(parts + [kv[:, KV_WIDTH:]], axis=1)
    inside = (i > 0) & (i < pl.num_programs(0) - 1)
    o_ref[...] = jnp.where(inside, out, 0.0).astype(BF16)


def _kv_projection(h_bf16, w_kv, layer, cos_pad, sin_pad):
    s, d = h_bf16.shape
    tile = BRANCH_TILE
    n_tiles = s // tile
    tab_spec = pl.BlockSpec((tile, HEAD_DIM), lambda i: (i, 0))
    return pl.pallas_call(
        _kv_kernel,
        grid=(n_tiles + 2,),
        in_specs=[pl.BlockSpec((tile, d), lambda i: (jnp.clip(i - 1, 0, n_tiles - 1), 0)),
                  _layer_resident(layer, (d, 2 * KV_WIDTH)), tab_spec, tab_spec],
        out_specs=pl.BlockSpec((tile, 2 * KV_WIDTH), lambda i: (i, 0)),
        out_shape=jax.ShapeDtypeStruct(((n_tiles + 2) * tile, 2 * KV_WIDTH), BF16),
        compiler_params=_params("parallel"),
        name="kv_projection",
    )(h_bf16, w_kv, cos_pad, sin_pad)


def _attn_branch_kernel(sink_ref, h_ref, w_ref, wc_ref, kv_ref, kvp_ref, kvn_ref, cos_ref, sin_ref, o_ref,
                        p0_ref, p1_ref, y0_ref, y1_ref):
    tile = BRANCH_TILE
    n_blocks = tile // BLOCK
    s = pl.program_id(0)
    n_tiles = 2 * (pl.num_programs(0) - 1)
    scale = HEAD_DIM ** -0.5

    @pl.when(s == 0)
    def _():
        p1_ref[...] = jnp.zeros_like(p1_ref)

    q_row = lax.broadcasted_iota(jnp.int32, (GROUP * BLOCK, 3 * BLOCK), 0) % BLOCK
    k_col = lax.broadcasted_iota(jnp.int32, (GROUP * BLOCK, 3 * BLOCK), 1)
    band = (k_col >= q_row) & (k_col <= q_row + 2 * BLOCK)
    head_of_row = lax.broadcasted_iota(jnp.int32, (GROUP * BLOCK, 1), 0) // BLOCK

    def band_rows(w, b, cols):
        first = w * n_blocks + b
        if first == 0:
            return jnp.concatenate([kvp_ref[:, cols], kv_ref[:2 * BLOCK, cols]], axis=0)
        if first == 2 * n_blocks - 1:
            return jnp.concatenate([kv_ref[(first - 1) * BLOCK:, cols], kvn_ref[:, cols]], axis=0)
        return kv_ref[(first - 1) * BLOCK:(first + 2) * BLOCK, cols]

    units = [(g, b) for g in range(N_KV_HEADS) for b in range(n_blocks)]
    job_cols = 2 * ATTN_WIDTH // len(units)
    out_cols = D_MODEL // len(units)

    def attend(p_ref, y_ref, w, tile_idx, mxu_jobs):
        rows = slice(w * tile, (w + 1) * tile)
        cos_q = cos_ref[rows, :]
        sin_q = sin_ref[rows, :]
        q_heads = {}

        def scores_of(g, b):
            if g not in q_heads:
                q_heads[g] = []
                for j in range(GROUP):
                    c0 = (g * GROUP + j) * HEAD_DIM
                    q = p_ref[:, c0:c0 + HEAD_DIM].astype(F32)
                    q_heads[g].append((_rope(q, cos_q, sin_q) * scale).astype(BF16))
            kb = band_rows(w, b, slice(g * HEAD_DIM, (g + 1) * HEAD_DIM))
            q4 = jnp.concatenate([qh[b * BLOCK:(b + 1) * BLOCK] for qh in q_heads[g]], axis=0)
            return lax.dot_general(q4, kb, (((1,), (1,)), ((), ())), preferred_element_type=F32)

        scores = scores_of(*units[0])
        for u, (g, b) in enumerate(units):
            mxu_jobs[u]()
            r = slice(b * BLOCK, (b + 1) * BLOCK)
            sink_col = jnp.zeros((GROUP * BLOCK, 1), F32)
            for j in range(GROUP):
                sink_col = jnp.where(head_of_row == j, sink_ref[g * GROUP + j], sink_col)
            valid = band
            if b == 0:
                valid = valid & (k_col >= jnp.where(tile_idx == 0, BLOCK, 0))
            if b == n_blocks - 1:
                valid = valid & (k_col < jnp.where(tile_idx == n_tiles - 1, 2 * BLOCK, 3 * BLOCK))
            scores = jnp.where(valid, scores, -jnp.inf)
            m = jnp.maximum(jnp.max(scores, axis=-1, keepdims=True), sink_col)
            p = jnp.exp(scores - m)
            denom = jnp.sum(p, axis=-1, keepdims=True) + jnp.exp(sink_col - m)
            if u + 1 < len(units):
                scores = scores_of(*units[u + 1])
            vb = band_rows(w, b, slice(KV_WIDTH + g * HEAD_DIM, KV_WIDTH + (g + 1) * HEAD_DIM))
            out = jnp.dot(p.astype(BF16), vb, preferred_element_type=F32) / denom
            for j in range(GROUP):
                c0 = (g * GROUP + j) * HEAD_DIM
                zj = p_ref[r, ATTN_WIDTH + c0:ATTN_WIDTH + c0 + HEAD_DIM].astype(F32)
                y_ref[r, c0:c0 + HEAD_DIM] = (out[j * BLOCK:(j + 1) * BLOCK] * _silu(zj)).astype(BF16)

    def project_job(h_rows, p_ref, u):
        def job():
            c = slice(u * job_cols, (u + 1) * job_cols)
            p_ref[:, c] = jnp.dot(h_ref[h_rows, :], w_ref[:, c], preferred_element_type=F32).astype(BF16)
        return job

    def branch_job(y_ref, out_rows, u):
        def job():
            c = slice(u * out_cols, (u + 1) * out_cols)
            o_ref[out_rows, c] = jnp.dot(y_ref[...], wc_ref[:, c], preferred_element_type=F32).astype(BF16)
        return job

    def both(*jobs):
        def job():
            for j in jobs:
                j()
        return job

    lo, hi = slice(0, tile), slice(tile, 2 * tile)
    n_units = len(units)
    attend(p1_ref, y1_ref, 0, 2 * s - 1, [project_job(lo, p0_ref, u) for u in range(n_units)])
    attend(p0_ref, y0_ref, 1, 2 * s,
           [both(project_job(hi, p1_ref, u), branch_job(y1_ref, lo, u)) for u in range(n_units)])
    for u in range(n_units):
        branch_job(y0_ref, hi, u)()


def _attention_branch(h_bf16, w_proj, w_branch, layer, sink, kv_pad, cos_pad, sin_pad):
    d = h_bf16.shape[1]
    tile = BRANCH_TILE
    halo_per_pair = 2 * tile // BLOCK
    n_halo = kv_pad.shape[0] // BLOCK
    kv_width = 2 * KV_WIDTH
    tab_spec = pl.BlockSpec((2 * tile, HEAD_DIM), lambda i: (i, 0))
    specs = [pl.BlockSpec((2 * tile, kv_width), lambda i: (i, 0)),
             pl.BlockSpec((BLOCK, kv_width), lambda i: (jnp.maximum(i * halo_per_pair - 1, 0), 0)),
             pl.BlockSpec((BLOCK, kv_width), lambda i: (jnp.minimum((i + 1) * halo_per_pair, n_halo - 1), 0)),
             tab_spec, tab_spec]
    s, _ = h_bf16.shape
    n_pairs = s // (2 * tile)
    return pl.pallas_call(
        _attn_branch_kernel,
        grid=(n_pairs + 1,),
        in_specs=[pl.BlockSpec(memory_space=pltpu.SMEM),
                  pl.BlockSpec((2 * tile, d), lambda i: (jnp.minimum(i, n_pairs - 1), 0)),
                  _layer_resident(layer, (d, 2 * ATTN_WIDTH)), _layer_resident(layer, (ATTN_WIDTH, d))] + specs,
        out_specs=pl.BlockSpec((2 * tile, d), lambda i: (i, 0)),
        out_shape=jax.ShapeDtypeStruct(((n_pairs + 1) * 2 * tile, d), BF16),
        scratch_shapes=[pltpu.VMEM((tile, 2 * ATTN_WIDTH), BF16), pltpu.VMEM((tile, 2 * ATTN_WIDTH), BF16),
                        pltpu.VMEM((tile, ATTN_WIDTH), BF16), pltpu.VMEM((tile, ATTN_WIDTH), BF16)],
        compiler_params=_params("arbitrary"),
        name="attention_branch",
    )(sink, h_bf16, w_proj, w_branch, kv_pad, kv_pad, kv_pad, cos_pad, sin_pad)


def _final_kernel(ca0_ref, ca1_ref, cb0_ref, cb1_ref, cc0_ref, cc1_ref, hb_ref, h_ref, wr_ref, wo_ref,
                  gb_ref, g_ref, b_ref, *out_refs):
    tile = BRANCH_TILE
    for t, (ca_ref, cb_ref, cc_ref) in enumerate(((ca0_ref, cb0_ref, cc0_ref), (ca1_ref, cb1_ref, cc1_ref))):
        rows = slice(t * tile, (t + 1) * tile)
        gate_c = _sigmoid(jnp.dot(hb_ref[rows, :], wr_ref[...], preferred_element_type=F32) + gb_ref[...])
        merged = ca_ref[...].astype(F32) + cb_ref[...].astype(F32) + gate_c * cc_ref[...].astype(F32)
        out = jnp.dot(merged.astype(BF16), wo_ref[...], preferred_element_type=F32)
        y = _layer_norm(ALPHA * h_ref[rows, :] + out, g_ref[...], b_ref[...])
        out_refs[0][rows, :] = y
        if len(out_refs) > 1:
            out_refs[1][rows, :] = y.astype(BF16)


def _merge_project_norm(contrib_a, contrib_b, contrib_c, h_bf16, h, w_gate_c, w_o, layer, gate_bias_c, ln_g, ln_b,
                        emit_bf16):
    s, d = h.shape
    tile = BRANCH_TILE
    row = pl.BlockSpec((2 * tile, d), lambda i: (i, 0))
    shifted = [pl.BlockSpec((tile, d), lambda i: (2 * i + 1, 0)), pl.BlockSpec((tile, d), lambda i: (2 * i + 2, 0))]
    out_specs = [row, row] if emit_bf16 else [row]
    out_shape = [jax.ShapeDtypeStruct((s, d), F32)] + ([jax.ShapeDtypeStruct((s, d), BF16)] if emit_bf16 else [])
    return pl.pallas_call(
        _final_kernel,
        grid=(s // (2 * tile),),
        in_specs=shifted * 3 + [row, row, _layer_resident(layer, (d, d)), _layer_resident(layer, (d, d)),
                                _resident((1, d)), _resident((1, d)), _resident((1, d))],
        out_specs=out_specs,
        out_shape=out_shape,
        compiler_params=_params("parallel"),
        name="merge_project_norm",
    )(contrib_a, contrib_a, contrib_b, contrib_b, contrib_c, contrib_c, h_bf16, h, w_gate_c, w_o,
      gate_bias_c.reshape(1, d), ln_g.reshape(1, d), ln_b.reshape(1, d))


def kernel(x, positions, ln0_g, ln0_b, w_in, conv_w, gmlp_ln_g, gmlp_ln_b, spatial_w, spatial_b, sink,
           w_branch_a, w_branch_b, w_branch_c, gate_b, w_out, ln_g, ln_b):
    bsz, s, d = x.shape
    assert (bsz, d) == (1, D_MODEL) and s % 1024 == 0 and w_in.shape == (DEPTH, D_MODEL, IN_WIDTH)

    def tiles(off, width):
        return tuple(range(off // WEIGHT_TILE, (off + width) // WEIGHT_TILE))

    w_conv = _gather_cast_columns(w_in, tiles(OFF_CONV, 4 * CONV_WIDTH) + tiles(OFF_GATE_A, d))
    w_gmlp = _gather_cast_columns(w_in, tiles(OFF_GMLP, 3 * GMLP_WIDTH) + tiles(OFF_GATE_B, d))
    w_qz = _gather_cast_columns(w_in, tiles(OFF_Q, ATTN_WIDTH) + tiles(OFF_ATTN_Z, ATTN_WIDTH))
    w_kv = _gather_cast_columns(w_in, tiles(OFF_KV, 2 * KV_WIDTH))
    w_gate_c = _gather_cast_columns(w_in, tiles(OFF_GATE_C, d))
    wa, wb, wc, wo = (_cast_bf16(w) for w in (w_branch_a, w_branch_b, w_branch_c, w_out))

    h, h_bf16 = _input_layer_norm(x.reshape(s, d), ln0_g, ln0_b)
    cos_pad, sin_pad = _rope_tables(jnp.pad(positions.reshape(s), (BRANCH_TILE, BRANCH_TILE)))
    for l in range(DEPTH):
        contrib_a = _conv_branch(h_bf16, w_conv, wa, l, conv_w[l], gate_b[l, 0])
        contrib_b = _gmlp_branch(h_bf16, w_gmlp, wb, l, gmlp_ln_g[l], gmlp_ln_b[l],
                                 spatial_w[l], spatial_b[l], gate_b[l, 1])
        kv_pad = _kv_projection(h_bf16, w_kv, l, cos_pad, sin_pad)
        contrib_c = _attention_branch(h_bf16, w_qz, wc, l, sink[l], kv_pad, cos_pad, sin_pad)
        outs = _merge_project_norm(contrib_a, contrib_b, contrib_c, h_bf16, h, w_gate_c, wo, l, gate_b[l, 2],
                                   ln_g[l], ln_b[l], emit_bf16=l + 1 < DEPTH)
        h = outs[0]
        h_bf16 = outs[1] if l + 1 < DEPTH else None
    return h.reshape(bsz, s, d)
```

```python
import jax
import jax.numpy as jnp
from jax import lax
from jax.experimental import pallas as pl
from jax.experimental.pallas import tpu as pltpu

D_MODEL = 2048
DEPTH = 2
HEAD_DIM = 128
N_Q_HEADS = 16
N_KV_HEADS = 4
GROUP = N_Q_HEADS // N_KV_HEADS
ATTN_WIDTH = N_Q_HEADS * HEAD_DIM
KV_WIDTH = N_KV_HEADS * HEAD_DIM
CONV_WIDTH = 1024
GMLP_WIDTH = 1024
GMLP_GROUPS = 8
GMLP_GROUP_DIM = GMLP_WIDTH // GMLP_GROUPS
CHUNK = 128
BLOCK = 128
ROPE_THETA = 500000.0
ROPE_DIM = HEAD_DIM // 4
ROPE_HALF = ROPE_DIM // 2
LN_EPS = 1e-5
ALPHA = (2.0 * DEPTH) ** 0.25
IN_WIDTH = 4 * CONV_WIDTH + 3 * GMLP_WIDTH + 2 * ATTN_WIDTH + 2 * KV_WIDTH + 3 * D_MODEL

OFF_CONV, OFF_GMLP, OFF_Q, OFF_KV, OFF_ATTN_Z = 0, 4096, 7168, 9216, 10240
OFF_GATE_A, OFF_GATE_B, OFF_GATE_C = 12288, 14336, 16384

WEIGHT_TILE = 1024
PROJ_CHUNK = 1024
BRANCH_TILE = 256
VMEM_LIMIT_BYTES = 56 * 1024 * 1024

F32 = jnp.float32
BF16 = jnp.bfloat16


def _params(*semantics):
    return pltpu.CompilerParams(dimension_semantics=semantics, vmem_limit_bytes=VMEM_LIMIT_BYTES)


def _silu(x):
    return x / (1.0 + jnp.exp(-x))


def _sigmoid(x):
    return 1.0 / (1.0 + jnp.exp(-x))


def _gelu_tanh(x):
    return 0.5 * x * (1.0 + jnp.tanh(0.7978845608028654 * (x + 0.044715 * (x * x * x))))


def _layer_norm(x, g, b):
    mu = jnp.mean(x, axis=-1, keepdims=True)
    xc = x - mu
    var = jnp.mean(xc * xc, axis=-1, keepdims=True)
    return xc * lax.rsqrt(var + LN_EPS) * g + b


def _resident(shape):
    return pl.BlockSpec(shape, lambda i: (0,) * len(shape), pipeline_mode=pl.Buffered(1))


def _layer_resident(layer, shape):
    return pl.BlockSpec((None,) + shape, lambda i: (layer,) + (0,) * len(shape), pipeline_mode=pl.Buffered(1))


def _rope_table_kernel(pos_ref, invf_ref, cos_ref, sin_ref):
    ang = pos_ref[...].astype(F32) * invf_ref[...]
    lane = lax.broadcasted_iota(jnp.int32, ang.shape, 1)
    c = jnp.cos(ang)
    s = jnp.sin(ang)
    cos_ref[...] = jnp.where(lane < ROPE_DIM, c, 1.0)
    sin_ref[...] = jnp.where(lane < ROPE_HALF, -s, jnp.where(lane < ROPE_DIM, s, 0.0))


def _rope_tables(positions, rows=512):
    s = positions.shape[0]
    inv_freq = ROPE_THETA ** (-jnp.arange(ROPE_HALF, dtype=F32) / ROPE_HALF)
    invf = jnp.tile(inv_freq, HEAD_DIM // ROPE_HALF).reshape(1, HEAD_DIM)
    tab = jax.ShapeDtypeStruct((s, HEAD_DIM), F32)
    tab_spec = pl.BlockSpec((rows, HEAD_DIM), lambda i: (i, 0))
    return pl.pallas_call(
        _rope_table_kernel,
        grid=(s // rows,),
        in_specs=[pl.BlockSpec((rows, 1), lambda i: (i, 0)), pl.BlockSpec((1, HEAD_DIM), lambda i: (0, 0))],
        out_specs=[tab_spec, tab_spec],
        out_shape=[tab, tab],
        compiler_params=_params("parallel"),
        name="rope_tables",
    )(positions.reshape(s, 1), invf)


def _rope(t, cos, sin):
    lane = lax.broadcasted_iota(jnp.int32, t.shape, 1)
    partner = jnp.where(lane < ROPE_HALF, pltpu.roll(t, HEAD_DIM - ROPE_HALF, 1), pltpu.roll(t, ROPE_HALF, 1))
    return t * cos + partner * sin


def _cast_kernel(tiles_ref, w_ref, o_ref):
    del tiles_ref
    o_ref[...] = w_ref[...].astype(o_ref.dtype)


def _gather_cast_columns(w, col_tiles, tile=WEIGHT_TILE):
    layers, k, _ = w.shape
    table = jnp.asarray(col_tiles, jnp.int32)
    return pl.pallas_call(
        _cast_kernel,
        grid_spec=pltpu.PrefetchScalarGridSpec(
            num_scalar_prefetch=1,
            grid=(layers, len(col_tiles)),
            in_specs=[pl.BlockSpec((None, k, tile), lambda l, j, t: (l, 0, t[j]))],
            out_specs=pl.BlockSpec((None, k, tile), lambda l, j, t: (l, 0, j)),
        ),
        out_shape=jax.ShapeDtypeStruct((layers, k, len(col_tiles) * tile), BF16),
        compiler_params=_params("parallel", "parallel"),
        name="gather_cast_columns",
    )(table, w)


def _cast_bf16(w):
    return _gather_cast_columns(w, tuple(range(w.shape[-1] // WEIGHT_TILE)))


def _project(h_ref, rows, w_ref, p_ref):
    h = h_ref[rows, :]
    for c in range(0, p_ref.shape[1], PROJ_CHUNK):
        p_ref[:, c:c + PROJ_CHUNK] = jnp.dot(
            h, w_ref[:, c:c + PROJ_CHUNK], preferred_element_type=F32).astype(BF16)


def _branch_call(body, name, h_bf16, w_proj, w_branch, layer, extra_inputs, extra_specs, y_width, out_width):
    s, d = h_bf16.shape
    tile = BRANCH_TILE
    n_pairs = s // (2 * tile)
    proj_width = w_proj.shape[2]
    return pl.pallas_call(
        body,
        grid=(n_pairs + 1,),
        in_specs=[pl.BlockSpec((2 * tile, d), lambda i: (jnp.minimum(i, n_pairs - 1), 0)),
                  _layer_resident(layer, (d, proj_width)), _layer_resident(layer, w_branch.shape[1:])] + extra_specs,
        out_specs=pl.BlockSpec((2 * tile, out_width), lambda i: (i, 0)),
        out_shape=jax.ShapeDtypeStruct(((n_pairs + 1) * 2 * tile, out_width), BF16),
        scratch_shapes=[pltpu.VMEM((tile, proj_width), BF16), pltpu.VMEM((tile, proj_width), BF16),
                        pltpu.VMEM((tile, y_width), BF16), pltpu.VMEM((tile, y_width), BF16)],
        compiler_params=_params("arbitrary"),
        name=name,
    )(h_bf16, w_proj, w_branch, *extra_inputs)


def _conv_branch_kernel(h_ref, w_ref, wa_ref, cw_ref, gb_ref, o_ref, p0_ref, p1_ref, y0_ref, y1_ref):
    tile = BRANCH_TILE
    s = pl.program_id(0)
    n_tiles = 2 * (pl.num_programs(0) - 1)
    cw = cw_ref[...]

    @pl.when(s == 0)
    def _():
        p0_ref[...] = jnp.zeros_like(p0_ref)
        p1_ref[...] = jnp.zeros_like(p1_ref)

    def gated_input_row(p_ref, row):
        c = p_ref[row:row + 1, CONV_WIDTH:2 * CONV_WIDTH].astype(F32)
        return c * p_ref[row:row + 1, 2 * CONV_WIDTH:3 * CONV_WIDTH].astype(F32)

    def mix(p_ref, y_ref, rows, tile_idx, y_prev, y_next):
        y = p_ref[:, CONV_WIDTH:2 * CONV_WIDTH].astype(F32) * p_ref[:, 2 * CONV_WIDTH:3 * CONV_WIDTH].astype(F32)
        y_prev = jnp.where(tile_idx > 0, y_prev, 0.0)
        y_next = jnp.where(tile_idx < n_tiles - 1, y_next, 0.0)
        row = lax.broadcasted_iota(jnp.int32, y.shape, 0)
        up = jnp.where(row == 0, y_prev, pltpu.roll(y, 1, 0))
        dn = jnp.where(row == tile - 1, y_next, pltpu.roll(y, tile - 1, 0))
        conv = cw[0:1, :] * up + cw[1:2, :] * y + cw[2:3, :] * dn
        z = p_ref[:, 3 * CONV_WIDTH:4 * CONV_WIDTH].astype(F32)
        y_ref[...] = (p_ref[:, :CONV_WIDTH].astype(F32) * conv * _silu(z)).astype(BF16)
        gate = _sigmoid(p_ref[:, 4 * CONV_WIDTH:].astype(F32) + gb_ref[...])
        o_ref[rows, :] = (gate * jnp.dot(y_ref[...], wa_ref[...], preferred_element_type=F32)).astype(BF16)

    lo, hi = slice(0, tile), slice(tile, 2 * tile)
    last_of_tile_before_p1 = gated_input_row(p0_ref, tile - 1)
    _project(h_ref, lo, w_ref, p0_ref)
    last_of_p1 = gated_input_row(p1_ref, tile - 1)
    mix(p1_ref, y1_ref, lo, 2 * s - 1, last_of_tile_before_p1, gated_input_row(p0_ref, 0))
    _project(h_ref, hi, w_ref, p1_ref)
    mix(p0_ref, y0_ref, hi, 2 * s, last_of_p1, gated_input_row(p1_ref, 0))


def _conv_branch(h_bf16, w_proj, w_branch, layer, conv_w, gate_bias):
    d = h_bf16.shape[1]
    return _branch_call(_conv_branch_kernel, "conv_branch", h_bf16, w_proj, w_branch, layer,
                        [conv_w, gate_bias.reshape(1, d)], [_resident((3, CONV_WIDTH)), _resident((1, d))],
                        CONV_WIDTH, d)


def _gmlp_branch_kernel(h_ref, w_ref, wb_ref, g_ref, b_ref, ws_ref, bias_ref, gb_ref, o_ref,
                        p0_ref, p1_ref, y0_ref, y1_ref):
    tile = BRANCH_TILE

    @pl.when(pl.program_id(0) == 0)
    def _():
        p1_ref[...] = jnp.zeros_like(p1_ref)

    proj_width = p0_ref.shape[1]
    n_jobs = 2 + tile // CHUNK * GMLP_GROUPS // 2
    job_cols = proj_width // n_jobs

    def mix(p_ref, y_ref, rows, h_rows, q_ref):
        jobs = [slice(u * job_cols, (u + 1) * job_cols) for u in range(n_jobs)]

        def run_job():
            c = jobs.pop(0)
            q_ref[:, c] = jnp.dot(h_ref[h_rows, :], w_ref[:, c], preferred_element_type=F32).astype(BF16)

        run_job()
        run_job()
        v = _layer_norm(_gelu_tanh(p_ref[:, GMLP_WIDTH:2 * GMLP_WIDTH].astype(F32)), g_ref[...], b_ref[...])
        v = v.astype(BF16)
        for c in range(tile // CHUNK):
            r = slice(c * CHUNK, (c + 1) * CHUNK)
            for g in range(GMLP_GROUPS):
                if g % 2 == 0:
                    run_job()
                cols = slice(g * GMLP_GROUP_DIM, (g + 1) * GMLP_GROUP_DIM)
                zcols = slice(2 * GMLP_WIDTH + g * GMLP_GROUP_DIM, 2 * GMLP_WIDTH + (g + 1) * GMLP_GROUP_DIM)
                mixed = jnp.dot(ws_ref[g], v[r, cols], preferred_element_type=F32) + bias_ref[:, cols]
                u = _gelu_tanh(p_ref[r, cols].astype(F32))
                y_ref[r, cols] = (u * mixed * _silu(p_ref[r, zcols].astype(F32))).astype(BF16)
        assert not jobs
        gate = _sigmoid(p_ref[:, 3 * GMLP_WIDTH:].astype(F32) + gb_ref[...])
        o_ref[rows, :] = (gate * jnp.dot(y_ref[...], wb_ref[...], preferred_element_type=F32)).astype(BF16)

    lo, hi = slice(0, tile), slice(tile, 2 * tile)
    mix(p1_ref, y1_ref, lo, lo, p0_ref)
    mix(p0_ref, y0_ref, hi, hi, p1_ref)


def _gmlp_branch(h_bf16, w_proj, w_branch, layer, ln_g, ln_b, spatial_w, spatial_b, gate_bias):
    d = h_bf16.shape[1]
    bias = jnp.repeat(spatial_b.T, GMLP_GROUP_DIM, axis=1)
    return _branch_call(
        _gmlp_branch_kernel, "gmlp_branch", h_bf16, w_proj, w_branch, layer,
        [ln_g.reshape(1, -1), ln_b.reshape(1, -1), spatial_w.astype(BF16), bias, gate_bias.reshape(1, d)],
        [_resident((1, GMLP_WIDTH)), _resident((1, GMLP_WIDTH)), _resident((GMLP_GROUPS, CHUNK, CHUNK)),
         _resident((CHUNK, GMLP_WIDTH)), _resident((1, d))],
        GMLP_WIDTH, d)


def _kv_kernel(h_ref, w_ref, cos_ref, sin_ref, o_ref):
    _kv_body(h_ref[...], w_ref, cos_ref, sin_ref, o_ref)


def _norm_kv_kernel(x_ref, g_ref, b_ref, w_ref, cos_ref, sin_ref, o_ref, hf_ref, hb_ref):
    h = _layer_norm(x_ref[...], g_ref[...], b_ref[...])
    hf_ref[...] = h
    hb = h.astype(BF16)
    hb_ref[...] = hb
    _kv_body(hb, w_ref, cos_ref, sin_ref, o_ref)


def _kv_body(h, w_ref, cos_ref, sin_ref, o_ref):
    i = pl.program_id(0)
    kv = jnp.dot(h, w_ref[...], preferred_element_type=F32)
    cos = cos_ref[...]
    sin = sin_ref[...]
    parts = [_rope(kv[:, g * HEAD_DIM:(g + 1) * HEAD_DIM], cos, sin) for g in range(N_KV_HEADS)]
    out = jnp.concatenate(parts + [kv[:, KV_WIDTH:]], axis=1)
    inside = (i > 0) & (i < pl.num_programs(0) - 1)
    o_ref[...] = jnp.where(inside, out, 0.0).astype(BF16)


def _kv_projection(h_or_x, w_kv, layer, cos_pad, sin_pad, input_norm=None):
    s, d = h_or_x.shape
    tile = BRANCH_TILE
    n_tiles = s // tile
    tab_spec = pl.BlockSpec((tile, HEAD_DIM), lambda i: (i, 0))
    row_spec = pl.BlockSpec((tile, d), lambda i: (jnp.clip(i - 1, 0, n_tiles - 1), 0))
    kv_spec = pl.BlockSpec((tile, 2 * KV_WIDTH), lambda i: (i, 0))
    kv_shape = jax.ShapeDtypeStruct(((n_tiles + 2) * tile, 2 * KV_WIDTH), BF16)
    common = [_layer_resident(layer, (d, 2 * KV_WIDTH)), tab_spec, tab_spec]
    if input_norm is None:
        return pl.pallas_call(
            _kv_kernel,
            grid=(n_tiles + 2,),
            in_specs=[row_spec] + common,
            out_specs=kv_spec,
            out_shape=kv_shape,
            compiler_params=_params("parallel"),
            name="kv_projection",
        )(h_or_x, w_kv, cos_pad, sin_pad)
    g, b = input_norm
    return pl.pallas_call(
        _norm_kv_kernel,
        grid=(n_tiles + 2,),
        in_specs=[row_spec, _resident((1, d)), _resident((1, d))] + common,
        out_specs=[kv_spec, row_spec, row_spec],
        out_shape=[kv_shape, jax.ShapeDtypeStruct((s, d), F32), jax.ShapeDtypeStruct((s, d), BF16)],
        compiler_params=_params("arbitrary"),
        name="norm_kv_projection",
    )(h_or_x, g.reshape(1, d), b.reshape(1, d), w_kv, cos_pad, sin_pad)


def _attn_branch_kernel(sink_ref, h_ref, w_ref, wc_ref, kv_ref, kvp_ref, kvn_ref, cos_ref, sin_ref, o_ref,
                        p0_ref, p1_ref, y0_ref, y1_ref):
    tile = BRANCH_TILE
    n_blocks = tile // BLOCK
    s = pl.program_id(0)
    n_tiles = 2 * (pl.num_programs(0) - 1)
    scale = HEAD_DIM ** -0.5

    @pl.when(s == 0)
    def _():
        p1_ref[...] = jnp.zeros_like(p1_ref)

    q_row = lax.broadcasted_iota(jnp.int32, (GROUP * BLOCK, 3 * BLOCK), 0) % BLOCK
    k_col = lax.broadcasted_iota(jnp.int32, (GROUP * BLOCK, 3 * BLOCK), 1)
    band = (k_col >= q_row) & (k_col <= q_row + 2 * BLOCK)
    head_of_row = lax.broadcasted_iota(jnp.int32, (GROUP * BLOCK, 1), 0) // BLOCK

    def band_rows(w, b, cols):
        first = w * n_blocks + b
        if first == 0:
            return jnp.concatenate([kvp_ref[:, cols], kv_ref[:2 * BLOCK, cols]], axis=0)
        if first == 2 * n_blocks - 1:
            return jnp.concatenate([kv_ref[(first - 1) * BLOCK:, cols], kvn_ref[:, cols]], axis=0)
        return kv_ref[(first - 1) * BLOCK:(first + 2) * BLOCK, cols]

    units = [(g, b) for g in range(N_KV_HEADS) for b in range(n_blocks)]
    job_cols = 2 * ATTN_WIDTH // len(units)
    out_cols = D_MODEL // len(units)

    def attend(p_ref, y_ref, w, tile_idx, mxu_jobs):
        rows = slice(w * tile, (w + 1) * tile)
        cos_q = cos_ref[rows, :]
        sin_q = sin_ref[rows, :]
        q_heads = {}

        def scores_of(g, b):
            if g not in q_heads:
                q_heads[g] = []
                for j in range(GROUP):
                    c0 = (g * GROUP + j) * HEAD_DIM
                    q = p_ref[:, c0:c0 + HEAD_DIM].astype(F32)
                    q_heads[g].append((_rope(q, cos_q, sin_q) * scale).astype(BF16))
            kb = band_rows(w, b, slice(g * HEAD_DIM, (g + 1) * HEAD_DIM))
            q4 = jnp.concatenate([qh[b * BLOCK:(b + 1) * BLOCK] for qh in q_heads[g]], axis=0)
            return lax.dot_general(q4, kb, (((1,), (1,)), ((), ())), preferred_element_type=F32)

        scores = scores_of(*units[0])
        for u, (g, b) in enumerate(units):
            mxu_jobs[u]()
            r = slice(b * BLOCK, (b + 1) * BLOCK)
            sink_col = jnp.zeros((GROUP * BLOCK, 1), F32)
            for j in range(GROUP):
                sink_col = jnp.where(head_of_row == j, sink_ref[g * GROUP + j], sink_col)
            valid = band
            if b == 0:
                valid = valid & (k_col >= jnp.where(tile_idx == 0, BLOCK, 0))
            if b == n_blocks - 1:
                valid = valid & (k_col < jnp.where(tile_idx == n_tiles - 1, 2 * BLOCK, 3 * BLOCK))
            scores = jnp.where(valid, scores, -jnp.inf)
            m = jnp.maximum(jnp.max(scores, axis=-1, keepdims=True), sink_col)
            p = jnp.exp(scores - m)
            denom = jnp.sum(p, axis=-1, keepdims=True) + jnp.exp(sink_col - m)
            if u + 1 < len(units):
                scores = scores_of(*units[u + 1])
            vb = band_rows(w, b, slice(KV_WIDTH + g * HEAD_DIM, KV_WIDTH + (g + 1) * HEAD_DIM))
            out = jnp.dot(p.astype(BF16), vb, preferred_element_type=F32) / denom
            for j in range(GROUP):
                c0 = (g * GROUP + j) * HEAD_DIM
                zj = p_ref[r, ATTN_WIDTH + c0:ATTN_WIDTH + c0 + HEAD_DIM].astype(F32)
                y_ref[r, c0:c0 + HEAD_DIM] = (out[j * BLOCK:(j + 1) * BLOCK] * _silu(zj)).astype(BF16)

    def project_job(h_rows, p_ref, u):
        def job():
            c = slice(u * job_cols, (u + 1) * job_cols)
            p_ref[:, c] = jnp.dot(h_ref[h_rows, :], w_ref[:, c], preferred_element_type=F32).astype(BF16)
        return job

    def branch_job(y_ref, out_rows, u):
        def job():
            c = slice(u * out_cols, (u + 1) * out_cols)
            o_ref[out_rows, c] = jnp.dot(y_ref[...], wc_ref[:, c], preferred_element_type=F32).astype(BF16)
        return job

    def both(*jobs):
        def job():
            for j in jobs:
                j()
        return job

    lo, hi = slice(0, tile), slice(tile, 2 * tile)
    n_units = len(units)
    attend(p1_ref, y1_ref, 0, 2 * s - 1, [project_job(lo, p0_ref, u) for u in range(n_units)])
    attend(p0_ref, y0_ref, 1, 2 * s,
           [both(project_job(hi, p1_ref, u), branch_job(y1_ref, lo, u)) for u in range(n_units)])
    for u in range(n_units):
        branch_job(y0_ref, hi, u)()


def _attention_branch(h_bf16, w_proj, w_branch, layer, sink, kv_pad, cos_pad, sin_pad):
    d = h_bf16.shape[1]
    tile = BRANCH_TILE
    halo_per_pair = 2 * tile // BLOCK
    n_halo = kv_pad.shape[0] // BLOCK
    kv_width = 2 * KV_WIDTH
    tab_spec = pl.BlockSpec((2 * tile, HEAD_DIM), lambda i: (i, 0))
    specs = [pl.BlockSpec((2 * tile, kv_width), lambda i: (i, 0)),
             pl.BlockSpec((BLOCK, kv_width), lambda i: (jnp.maximum(i * halo_per_pair - 1, 0), 0)),
             pl.BlockSpec((BLOCK, kv_width), lambda i: (jnp.minimum((i + 1) * halo_per_pair, n_halo - 1), 0)),
             tab_spec, tab_spec]
    s, _ = h_bf16.shape
    n_pairs = s // (2 * tile)
    return pl.pallas_call(
        _attn_branch_kernel,
        grid=(n_pairs + 1,),
        in_specs=[pl.BlockSpec(memory_space=pltpu.SMEM),
                  pl.BlockSpec((2 * tile, d), lambda i: (jnp.minimum(i, n_pairs - 1), 0)),
                  _layer_resident(layer, (d, 2 * ATTN_WIDTH)), _layer_resident(layer, (ATTN_WIDTH, d))] + specs,
        out_specs=pl.BlockSpec((2 * tile, d), lambda i: (i, 0)),
        out_shape=jax.ShapeDtypeStruct(((n_pairs + 1) * 2 * tile, d), BF16),
        scratch_shapes=[pltpu.VMEM((tile, 2 * ATTN_WIDTH), BF16), pltpu.VMEM((tile, 2 * ATTN_WIDTH), BF16),
                        pltpu.VMEM((tile, ATTN_WIDTH), BF16), pltpu.VMEM((tile, ATTN_WIDTH), BF16)],
        compiler_params=_params("arbitrary"),
        name="attention_branch",
    )(sink, h_bf16, w_proj, w_branch, kv_pad, kv_pad, kv_pad, cos_pad, sin_pad)


def _final_kernel(ca0_ref, ca1_ref, cb0_ref, cb1_ref, cc0_ref, cc1_ref, hb_ref, h_ref, wr_ref, wo_ref,
                  gb_ref, g_ref, b_ref, *out_refs):
    tile = BRANCH_TILE
    for t, (ca_ref, cb_ref, cc_ref) in enumerate(((ca0_ref, cb0_ref, cc0_ref), (ca1_ref, cb1_ref, cc1_ref))):
        rows = slice(t * tile, (t + 1) * tile)
        gate_c = _sigmoid(jnp.dot(hb_ref[rows, :], wr_ref[...], preferred_element_type=F32) + gb_ref[...])
        merged = ca_ref[...].astype(F32) + cb_ref[...].astype(F32) + gate_c * cc_ref[...].astype(F32)
        out = jnp.dot(merged.astype(BF16), wo_ref[...], preferred_element_type=F32)
        y = _layer_norm(ALPHA * h_ref[rows, :] + out, g_ref[...], b_ref[...])
        out_refs[0][rows, :] = y
        if len(out_refs) > 1:
            out_refs[1][rows, :] = y.astype(BF16)


def _merge_project_norm(contrib_a, contrib_b, contrib_c, h_bf16, h, w_gate_c, w_o, layer, gate_bias_c, ln_g, ln_b,
                        emit_bf16):
    s, d = h.shape
    tile = BRANCH_TILE
    row = pl.BlockSpec((2 * tile, d), lambda i: (i, 0))
    shifted = [pl.BlockSpec((tile, d), lambda i: (2 * i + 1, 0)), pl.BlockSpec((tile, d), lambda i: (2 * i + 2, 0))]
    out_specs = [row, row] if emit_bf16 else [row]
    out_shape = [jax.ShapeDtypeStruct((s, d), F32)] + ([jax.ShapeDtypeStruct((s, d), BF16)] if emit_bf16 else [])
    return pl.pallas_call(
        _final_kernel,
        grid=(s // (2 * tile),),
        in_specs=shifted * 3 + [row, row, _layer_resident(layer, (d, d)), _layer_resident(layer, (d, d)),
                                _resident((1, d)), _resident((1, d)), _resident((1, d))],
        out_specs=out_specs,
        out_shape=out_shape,
        compiler_params=_params("parallel"),
        name="merge_project_norm",
    )(contrib_a, contrib_a, contrib_b, contrib_b, contrib_c, contrib_c, h_bf16, h, w_gate_c, w_o,
      gate_bias_c.reshape(1, d), ln_g.reshape(1, d), ln_b.reshape(1, d))


def kernel(x, positions, ln0_g, ln0_b, w_in, conv_w, gmlp_ln_g, gmlp_ln_b, spatial_w, spatial_b, sink,
           w_branch_a, w_branch_b, w_branch_c, gate_b, w_out, ln_g, ln_b):
    bsz, s, d = x.shape
    assert (bsz, d) == (1, D_MODEL) and s % 1024 == 0 and w_in.shape == (DEPTH, D_MODEL, IN_WIDTH)

    def tiles(off, width):
        return tuple(range(off // WEIGHT_TILE, (off + width) // WEIGHT_TILE))

    w_conv = _gather_cast_columns(w_in, tiles(OFF_CONV, 4 * CONV_WIDTH) + tiles(OFF_GATE_A, d))
    w_gmlp = _gather_cast_columns(w_in, tiles(OFF_GMLP, 3 * GMLP_WIDTH) + tiles(OFF_GATE_B, d))
    w_qz = _gather_cast_columns(w_in, tiles(OFF_Q, ATTN_WIDTH) + tiles(OFF_ATTN_Z, ATTN_WIDTH))
    w_kv = _gather_cast_columns(w_in, tiles(OFF_KV, 2 * KV_WIDTH))
    w_gate_c = _gather_cast_columns(w_in, tiles(OFF_GATE_C, d))
    wa, wb, wc, wo = (_cast_bf16(w) for w in (w_branch_a, w_branch_b, w_branch_c, w_out))

    cos_pad, sin_pad = _rope_tables(jnp.pad(positions.reshape(s), (BRANCH_TILE, BRANCH_TILE)))
    h = h_bf16 = None
    for l in range(DEPTH):
        if l == 0:
            kv_pad, h, h_bf16 = _kv_projection(x.reshape(s, d), w_kv, l, cos_pad, sin_pad, (ln0_g, ln0_b))
        else:
            kv_pad = _kv_projection(h_bf16, w_kv, l, cos_pad, sin_pad)
        contrib_a = _conv_branch(h_bf16, w_conv, wa, l, conv_w[l], gate_b[l, 0])
        contrib_b = _gmlp_branch(h_bf16, w_gmlp, wb, l, gmlp_ln_g[l], gmlp_ln_b[l],
                                 spatial_w[l], spatial_b[l], gate_b[l, 1])
        contrib_c = _attention_branch(h_bf16, w_qz, wc, l, sink[l], kv_pad, cos_pad, sin_pad)
        outs = _merge_project_norm(contrib_a, contrib_b, contrib_c, h_bf16, h, w_gate_c, wo, l, gate_b[l, 2],
                                   ln_g[l], ln_b[l], emit_bf16=l + 1 < DEPTH)
        h = outs[0]
        h_bf16 = outs[1] if l + 1 < DEPTH else None
    return h.reshape(bsz, s, d)
```

```python
import functools

import jax
import jax.numpy as jnp
from jax import lax
from jax.experimental import pallas as pl
from jax.experimental.pallas import tpu as pltpu

D_MODEL = 2048
DEPTH = 2
HEAD_DIM = 128
N_Q_HEADS = 16
N_KV_HEADS = 4
GROUP = N_Q_HEADS // N_KV_HEADS
ATTN_WIDTH = N_Q_HEADS * HEAD_DIM
KV_WIDTH = N_KV_HEADS * HEAD_DIM
CONV_WIDTH = 1024
GMLP_WIDTH = 1024
GMLP_GROUPS = 8
GMLP_GROUP_DIM = GMLP_WIDTH // GMLP_GROUPS
CHUNK = 128
BLOCK = 128
ROPE_THETA = 500000.0
ROPE_DIM = HEAD_DIM // 4
ROPE_HALF = ROPE_DIM // 2
LN_EPS = 1e-5
LOG2_E = 1.4426950408889634
ALPHA = (2.0 * DEPTH) ** 0.25
IN_WIDTH = 4 * CONV_WIDTH + 3 * GMLP_WIDTH + 2 * ATTN_WIDTH + 2 * KV_WIDTH + 3 * D_MODEL

OFF_CONV, OFF_GMLP, OFF_Q, OFF_KV, OFF_ATTN_Z = 0, 4096, 7168, 9216, 10240
OFF_GATE_A, OFF_GATE_B, OFF_GATE_C = 12288, 14336, 16384

PROJ_CHUNK = 1024
BRANCH_TILE = 256
UNITS_PER_JOB = 2
VMEM_LIMIT_BYTES = 60 * 1024 * 1024

F32 = jnp.float32
BF16 = jnp.bfloat16


def _params(*semantics):
    return pltpu.CompilerParams(dimension_semantics=semantics, vmem_limit_bytes=VMEM_LIMIT_BYTES)


def _silu(x):
    return x / (1.0 + jnp.exp(-x))


def _sigmoid(x):
    return 1.0 / (1.0 + jnp.exp(-x))


def _gelu_tanh(x):
    return 0.5 * x * (1.0 + jnp.tanh(0.7978845608028654 * (x + 0.044715 * (x * x * x))))


def _layer_norm(x, g, b):
    mu = jnp.mean(x, axis=-1, keepdims=True)
    xc = x - mu
    var = jnp.mean(xc * xc, axis=-1, keepdims=True)
    return xc * lax.rsqrt(var + LN_EPS) * g + b


def _resident(shape):
    return pl.BlockSpec(shape, lambda i: (0,) * len(shape), pipeline_mode=pl.Buffered(1))


def _rope_table_kernel(pos_ref, invf_ref, cos_ref, sin_ref):
    ang = pos_ref[...].astype(F32) * invf_ref[...]
    lane = lax.broadcasted_iota(jnp.int32, ang.shape, 1)
    c = jnp.cos(ang)
    s = jnp.sin(ang)
    cos_ref[...] = jnp.where(lane < ROPE_DIM, c, 1.0)
    sin_ref[...] = jnp.where(lane < ROPE_HALF, -s, jnp.where(lane < ROPE_DIM, s, 0.0))


def _rope_tables(positions, rows=512):
    s = positions.shape[0]
    inv_freq = ROPE_THETA ** (-jnp.arange(ROPE_HALF, dtype=F32) / ROPE_HALF)
    invf = jnp.tile(inv_freq, HEAD_DIM // ROPE_HALF).reshape(1, HEAD_DIM)
    tab = jax.ShapeDtypeStruct((s, HEAD_DIM), F32)
    tab_spec = pl.BlockSpec((rows, HEAD_DIM), lambda i: (i, 0))
    return pl.pallas_call(
        _rope_table_kernel,
        grid=(s // rows,),
        in_specs=[pl.BlockSpec((rows, 1), lambda i: (i, 0)), pl.BlockSpec((1, HEAD_DIM), lambda i: (0, 0))],
        out_specs=[tab_spec, tab_spec],
        out_shape=[tab, tab],
        compiler_params=_params("parallel"),
        name="rope_tables",
    )(positions.reshape(s, 1), invf)


def _rope(t, cos, sin):
    lane = lax.broadcasted_iota(jnp.int32, t.shape, 1)
    partner = jnp.where(lane < ROPE_HALF, pltpu.roll(t, HEAD_DIM - ROPE_HALF, 1), pltpu.roll(t, ROPE_HALF, 1))
    return t * cos + partner * sin


WEIGHT_CHUNK = 128
HBM = pl.BlockSpec(memory_space=pl.ANY)


def _column_chunks(*segments):
    return tuple(c for off, width in segments for c in range(off, off + width, WEIGHT_CHUNK))


def _weight_scratch(k, n_chunks):
    return pltpu.VMEM((k, n_chunks * WEIGHT_CHUNK), BF16)


def _staging_scratch(k):
    return [pltpu.VMEM((2, k, WEIGHT_CHUNK), F32), pltpu.SemaphoreType.DMA((2,))]


def _load_weights(src_hbm, layer, chunk_cols, dst_ref, stage_ref, sem_ref):
    k = dst_ref.shape[0]

    def copy(i):
        slot = i % 2
        return pltpu.make_async_copy(src_hbm.at[layer, :, pl.ds(chunk_cols[i], WEIGHT_CHUNK)],
                                     stage_ref.at[slot, pl.ds(0, k), :], sem_ref.at[slot])

    copy(0).start()
    for i in range(len(chunk_cols)):
        if i + 1 < len(chunk_cols):
            copy(i + 1).start()
        copy(i).wait()
        dst_ref[:, i * WEIGHT_CHUNK:(i + 1) * WEIGHT_CHUNK] = stage_ref[i % 2, :k, :].astype(BF16)


def _project(h_ref, rows, w_ref, p_ref):
    h = h_ref[rows, :]
    for c in range(0, p_ref.shape[1], PROJ_CHUNK):
        p_ref[:, c:c + PROJ_CHUNK] = jnp.dot(
            h, w_ref[:, c:c + PROJ_CHUNK], preferred_element_type=F32).astype(BF16)


def _branch_call(body, name, h_bf16, w_in, proj_chunks, w_branch, layer, extra_inputs, extra_specs, y_width,
                 out_width):
    s, d = h_bf16.shape
    tile = BRANCH_TILE
    n_pairs = s // (2 * tile)
    proj_width = len(proj_chunks) * WEIGHT_CHUNK
    branch_k, branch_n = w_branch.shape[1:]
    branch_chunks = _column_chunks((0, branch_n))
    n_extra = len(extra_inputs)

    def kernel_fn(w_in_hbm, w_branch_hbm, h_ref, *rest):
        extra_refs, o_ref = rest[:n_extra], rest[n_extra]
        w_ref, wb_ref, stage_ref, sem_ref, p0_ref, p1_ref, y0_ref, y1_ref = rest[n_extra + 1:]

        @pl.when(pl.program_id(0) == 0)
        def _():
            _load_weights(w_in_hbm, layer, proj_chunks, w_ref, stage_ref, sem_ref)
            _load_weights(w_branch_hbm, layer, branch_chunks, wb_ref, stage_ref, sem_ref)

        body(h_ref, w_ref, wb_ref, *extra_refs, o_ref, p0_ref, p1_ref, y0_ref, y1_ref)

    return pl.pallas_call(
        kernel_fn,
        grid=(n_pairs + 1,),
        in_specs=[HBM, HBM, pl.BlockSpec((2 * tile, d), lambda i: (jnp.minimum(i, n_pairs - 1), 0))] + extra_specs,
        out_specs=pl.BlockSpec((2 * tile, out_width), lambda i: (i, 0)),
        out_shape=jax.ShapeDtypeStruct(((n_pairs + 1) * 2 * tile, out_width), BF16),
        scratch_shapes=[_weight_scratch(d, len(proj_chunks)), _weight_scratch(branch_k, len(branch_chunks))]
        + _staging_scratch(d)
        + [pltpu.VMEM((tile, proj_width), BF16), pltpu.VMEM((tile, proj_width), BF16),
           pltpu.VMEM((tile, y_width), BF16), pltpu.VMEM((tile, y_width), BF16)],
        compiler_params=_params("arbitrary"),
        name=name,
    )(w_in, w_branch, h_bf16, *extra_inputs)


def _conv_branch_kernel(h_ref, w_ref, wa_ref, cw_ref, gb_ref, o_ref, p0_ref, p1_ref, y0_ref, y1_ref):
    tile = BRANCH_TILE
    s = pl.program_id(0)
    n_tiles = 2 * (pl.num_programs(0) - 1)
    cw = cw_ref[...]

    @pl.when(s == 0)
    def _():
        p0_ref[...] = jnp.zeros_like(p0_ref)
        p1_ref[...] = jnp.zeros_like(p1_ref)

    def gated_input_row(p_ref, row):
        c = p_ref[row:row + 1, CONV_WIDTH:2 * CONV_WIDTH].astype(F32)
        return c * p_ref[row:row + 1, 2 * CONV_WIDTH:3 * CONV_WIDTH].astype(F32)

    def mix(p_ref, y_ref, rows, tile_idx, y_prev, y_next):
        y = p_ref[:, CONV_WIDTH:2 * CONV_WIDTH].astype(F32) * p_ref[:, 2 * CONV_WIDTH:3 * CONV_WIDTH].astype(F32)
        y_prev = jnp.where(tile_idx > 0, y_prev, 0.0)
        y_next = jnp.where(tile_idx < n_tiles - 1, y_next, 0.0)
        row = lax.broadcasted_iota(jnp.int32, y.shape, 0)
        up = jnp.where(row == 0, y_prev, pltpu.roll(y, 1, 0))
        dn = jnp.where(row == tile - 1, y_next, pltpu.roll(y, tile - 1, 0))
        conv = cw[0:1, :] * up + cw[1:2, :] * y + cw[2:3, :] * dn
        z = p_ref[:, 3 * CONV_WIDTH:4 * CONV_WIDTH].astype(F32)
        y_ref[...] = (p_ref[:, :CONV_WIDTH].astype(F32) * conv * _silu(z)).astype(BF16)
        gate = _sigmoid(p_ref[:, 4 * CONV_WIDTH:].astype(F32) + gb_ref[...])
        o_ref[rows, :] = (gate * jnp.dot(y_ref[...], wa_ref[...], preferred_element_type=F32)).astype(BF16)

    lo, hi = slice(0, tile), slice(tile, 2 * tile)
    last_of_tile_before_p1 = gated_input_row(p0_ref, tile - 1)
    _project(h_ref, lo, w_ref, p0_ref)
    last_of_p1 = gated_input_row(p1_ref, tile - 1)
    mix(p1_ref, y1_ref, lo, 2 * s - 1, last_of_tile_before_p1, gated_input_row(p0_ref, 0))
    _project(h_ref, hi, w_ref, p1_ref)
    mix(p0_ref, y0_ref, hi, 2 * s, last_of_p1, gated_input_row(p1_ref, 0))


def _conv_branch(h_bf16, w_in, w_branch, layer, conv_w, gate_bias):
    d = h_bf16.shape[1]
    proj_chunks = _column_chunks((OFF_CONV, 4 * CONV_WIDTH), (OFF_GATE_A, d))
    return _branch_call(_conv_branch_kernel, "conv_branch", h_bf16, w_in, proj_chunks, w_branch, layer,
                        [conv_w, gate_bias.reshape(1, d)], [_resident((3, CONV_WIDTH)), _resident((1, d))],
                        CONV_WIDTH, d)


def _gmlp_branch_kernel(h_ref, w_ref, wb_ref, g_ref, b_ref, ws_ref, bias_ref, gb_ref, acc_ref, o_ref,
                        p0_ref, p1_ref, y0_ref, y1_ref):
    tile = BRANCH_TILE

    @pl.when(pl.program_id(0) == 0)
    def _():
        p1_ref[...] = jnp.zeros_like(p1_ref)

    proj_width = p0_ref.shape[1]
    n_jobs = 2 + tile // CHUNK * GMLP_GROUPS // 2
    job_cols = proj_width // n_jobs

    def mix(p_ref, y_ref, rows, h_rows, q_ref):
        jobs = [slice(u * job_cols, (u + 1) * job_cols) for u in range(n_jobs)]

        def run_job():
            c = jobs.pop(0)
            q_ref[:, c] = jnp.dot(h_ref[h_rows, :], w_ref[:, c], preferred_element_type=F32).astype(BF16)

        run_job()
        run_job()
        v = _layer_norm(_gelu_tanh(p_ref[:, GMLP_WIDTH:2 * GMLP_WIDTH].astype(F32)), g_ref[...], b_ref[...])
        v = v.astype(BF16)
        for c in range(tile // CHUNK):
            r = slice(c * CHUNK, (c + 1) * CHUNK)
            for g in range(GMLP_GROUPS):
                if g % 2 == 0:
                    run_job()
                cols = slice(g * GMLP_GROUP_DIM, (g + 1) * GMLP_GROUP_DIM)
                zcols = slice(2 * GMLP_WIDTH + g * GMLP_GROUP_DIM, 2 * GMLP_WIDTH + (g + 1) * GMLP_GROUP_DIM)
                mixed = jnp.dot(ws_ref[g], v[r, cols], preferred_element_type=F32) + bias_ref[:, cols]
                u = _gelu_tanh(p_ref[r, cols].astype(F32))
                y_ref[r, cols] = (u * mixed * _silu(p_ref[r, zcols].astype(F32))).astype(BF16)
        assert not jobs
        gate = _sigmoid(p_ref[:, 3 * GMLP_WIDTH:].astype(F32) + gb_ref[...])
        contrib = gate * jnp.dot(y_ref[...], wb_ref[...], preferred_element_type=F32)
        o_ref[rows, :] = (acc_ref[rows, :].astype(F32) + contrib).astype(BF16)

    lo, hi = slice(0, tile), slice(tile, 2 * tile)
    mix(p1_ref, y1_ref, lo, lo, p0_ref)
    mix(p0_ref, y0_ref, hi, hi, p1_ref)


def _gmlp_branch(h_bf16, w_in, w_branch, layer, ln_g, ln_b, spatial_w, spatial_b, gate_bias, accumulate_onto):
    d = h_bf16.shape[1]
    bias = jnp.repeat(spatial_b.T, GMLP_GROUP_DIM, axis=1)
    proj_chunks = _column_chunks((OFF_GMLP, 3 * GMLP_WIDTH), (OFF_GATE_B, d))
    return _branch_call(
        _gmlp_branch_kernel, "gmlp_branch", h_bf16, w_in, proj_chunks, w_branch, layer,
        [ln_g.reshape(1, -1), ln_b.reshape(1, -1), spatial_w.astype(BF16), bias, gate_bias.reshape(1, d),
         accumulate_onto],
        [_resident((1, GMLP_WIDTH)), _resident((1, GMLP_WIDTH)), _resident((GMLP_GROUPS, CHUNK, CHUNK)),
         _resident((CHUNK, GMLP_WIDTH)), _resident((1, d)),
         pl.BlockSpec((2 * BRANCH_TILE, d), lambda i: (i, 0))],
        GMLP_WIDTH, d)


KV_CHUNKS = _column_chunks((OFF_KV, 2 * KV_WIDTH))


def _kv_kernel(layer, w_in_hbm, h_ref, cos_ref, sin_ref, o_ref, w_ref, stage_ref, sem_ref):
    _kv_body(layer, w_in_hbm, h_ref[...], cos_ref, sin_ref, o_ref, w_ref, stage_ref, sem_ref)


def _norm_kv_kernel(layer, w_in_hbm, x_ref, g_ref, b_ref, cos_ref, sin_ref, o_ref, hf_ref, hb_ref,
                    w_ref, stage_ref, sem_ref):
    h = _layer_norm(x_ref[...], g_ref[...], b_ref[...])
    hf_ref[...] = h
    hb = h.astype(BF16)
    hb_ref[...] = hb
    _kv_body(layer, w_in_hbm, hb, cos_ref, sin_ref, o_ref, w_ref, stage_ref, sem_ref)


def _kv_body(layer, w_in_hbm, h, cos_ref, sin_ref, o_ref, w_ref, stage_ref, sem_ref):
    i = pl.program_id(0)

    @pl.when(i == 0)
    def _():
        _load_weights(w_in_hbm, layer, KV_CHUNKS, w_ref, stage_ref, sem_ref)

    kv = jnp.dot(h, w_ref[...], preferred_element_type=F32)
    cos = cos_ref[...]
    sin = sin_ref[...]
    parts = [_rope(kv[:, g * HEAD_DIM:(g + 1) * HEAD_DIM], cos, sin) for g in range(N_KV_HEADS)]
    out = jnp.concatenate(parts + [kv[:, KV_WIDTH:]], axis=1)
    inside = (i > 0) & (i < pl.num_programs(0) - 1)
    o_ref[...] = jnp.where(inside, out, 0.0).astype(BF16)


def _kv_projection(h_or_x, w_in, layer, cos_pad, sin_pad, input_norm=None):
    s, d = h_or_x.shape
    tile = BRANCH_TILE
    n_tiles = s // tile
    tab_spec = pl.BlockSpec((tile, HEAD_DIM), lambda i: (i, 0))
    row_spec = pl.BlockSpec((tile, d), lambda i: (jnp.clip(i - 1, 0, n_tiles - 1), 0))
    kv_spec = pl.BlockSpec((tile, 2 * KV_WIDTH), lambda i: (i, 0))
    kv_shape = jax.ShapeDtypeStruct(((n_tiles + 2) * tile, 2 * KV_WIDTH), BF16)
    scratch = [_weight_scratch(d, len(KV_CHUNKS))] + _staging_scratch(d)
    if input_norm is None:
        return pl.pallas_call(
            functools.partial(_kv_kernel, layer),
            grid=(n_tiles + 2,),
            in_specs=[HBM, row_spec, tab_spec, tab_spec],
            out_specs=kv_spec,
            out_shape=kv_shape,
            scratch_shapes=scratch,
            compiler_params=_params("arbitrary"),
            name="kv_projection",
        )(w_in, h_or_x, cos_pad, sin_pad)
    g, b = input_norm
    return pl.pallas_call(
        functools.partial(_norm_kv_kernel, layer),
        grid=(n_tiles + 2,),
        in_specs=[HBM, row_spec, _resident((1, d)), _resident((1, d)), tab_spec, tab_spec],
        out_specs=[kv_spec, row_spec, row_spec],
        out_shape=[kv_shape, jax.ShapeDtypeStruct((s, d), F32), jax.ShapeDtypeStruct((s, d), BF16)],
        scratch_shapes=scratch,
        compiler_params=_params("arbitrary"),
        name="norm_kv_projection",
    )(w_in, h_or_x, g.reshape(1, d), b.reshape(1, d), cos_pad, sin_pad)


def _attn_branch_kernel(h_ref, w_ref, wc_ref, sink_ref, kv_ref, kvp_ref, kvn_ref, cos_ref, sin_ref, o_ref,
                        p0_ref, p1_ref, y0_ref, y1_ref):
    tile = BRANCH_TILE
    n_blocks = tile // BLOCK
    s = pl.program_id(0)
    n_tiles = 2 * (pl.num_programs(0) - 1)
    scale = HEAD_DIM ** -0.5 * LOG2_E

    @pl.when(s == 0)
    def _():
        p1_ref[...] = jnp.zeros_like(p1_ref)

    q_row = lax.broadcasted_iota(jnp.int32, (GROUP * BLOCK, 3 * BLOCK), 0) % BLOCK
    k_col = lax.broadcasted_iota(jnp.int32, (GROUP * BLOCK, 3 * BLOCK), 1)
    band = (k_col >= q_row) & (k_col <= q_row + 2 * BLOCK)
    head_of_row = lax.broadcasted_iota(jnp.int32, (GROUP * BLOCK, 1), 0) // BLOCK

    def band_rows(w, b, cols):
        first = w * n_blocks + b
        if first == 0:
            return jnp.concatenate([kvp_ref[:, cols], kv_ref[:2 * BLOCK, cols]], axis=0)
        if first == 2 * n_blocks - 1:
            return jnp.concatenate([kv_ref[(first - 1) * BLOCK:, cols], kvn_ref[:, cols]], axis=0)
        return kv_ref[(first - 1) * BLOCK:(first + 2) * BLOCK, cols]

    units = [(g, b) for g in range(N_KV_HEADS) for b in range(n_blocks)]
    n_jobs = len(units) // UNITS_PER_JOB
    job_cols = 2 * ATTN_WIDTH // n_jobs
    out_cols = D_MODEL // n_jobs

    def attend(p_ref, y_ref, w, tile_idx, mxu_jobs):
        rows = slice(w * tile, (w + 1) * tile)
        cos_q = cos_ref[rows, :]
        sin_q = sin_ref[rows, :]
        q_heads = {}

        def scores_of(g, b):
            if g not in q_heads:
                q_heads[g] = []
                for j in range(GROUP):
                    c0 = (g * GROUP + j) * HEAD_DIM
                    q = p_ref[:, c0:c0 + HEAD_DIM].astype(F32)
                    q_heads[g].append((_rope(q, cos_q, sin_q) * scale).astype(BF16))
            kb = band_rows(w, b, slice(g * HEAD_DIM, (g + 1) * HEAD_DIM))
            q4 = jnp.concatenate([qh[b * BLOCK:(b + 1) * BLOCK] for qh in q_heads[g]], axis=0)
            return lax.dot_general(q4, kb, (((1,), (1,)), ((), ())), preferred_element_type=F32)

        scores = scores_of(*units[0])
        for u, (g, b) in enumerate(units):
            mxu_jobs[u]()
            r = slice(b * BLOCK, (b + 1) * BLOCK)
            sink_col = jnp.zeros((GROUP * BLOCK, 1), F32)
            for j in range(GROUP):
                sink_col = jnp.where(head_of_row == j, sink_ref[g * GROUP + j] * LOG2_E, sink_col)
            valid = band
            if b == 0:
                valid = valid & (k_col >= jnp.where(tile_idx == 0, BLOCK, 0))
            if b == n_blocks - 1:
                valid = valid & (k_col < jnp.where(tile_idx == n_tiles - 1, 2 * BLOCK, 3 * BLOCK))
            scores = jnp.where(valid, scores, -jnp.inf)
            m = jnp.maximum(jnp.max(scores, axis=-1, keepdims=True), sink_col)
            p = jnp.exp2(scores - m)
            denom = jnp.sum(p, axis=-1, keepdims=True) + jnp.exp2(sink_col - m)
            if u + 1 < len(units):
                scores = scores_of(*units[u + 1])
            vb = band_rows(w, b, slice(KV_WIDTH + g * HEAD_DIM, KV_WIDTH + (g + 1) * HEAD_DIM))
            out = jnp.dot(p.astype(BF16), vb, preferred_element_type=F32) / denom
            for j in range(GROUP):
                c0 = (g * GROUP + j) * HEAD_DIM
                zj = p_ref[r, ATTN_WIDTH + c0:ATTN_WIDTH + c0 + HEAD_DIM].astype(F32)
                y_ref[r, c0:c0 + HEAD_DIM] = (out[j * BLOCK:(j + 1) * BLOCK] * _silu(zj)).astype(BF16)

    def project_job(h_rows, p_ref, u):
        def job():
            c = slice(u * job_cols, (u + 1) * job_cols)
            p_ref[:, c] = jnp.dot(h_ref[h_rows, :], w_ref[:, c], preferred_element_type=F32).astype(BF16)
        return job

    def branch_job(y_ref, out_rows, u):
        def job():
            c = slice(u * out_cols, (u + 1) * out_cols)
            o_ref[out_rows, c] = jnp.dot(y_ref[...], wc_ref[:, c], preferred_element_type=F32).astype(BF16)
        return job

    def both(*jobs):
        def job():
            for j in jobs:
                j()
        return job

    def spread(jobs):
        return [jobs[u // UNITS_PER_JOB] if u % UNITS_PER_JOB == 0 else both() for u in range(len(units))]

    lo, hi = slice(0, tile), slice(tile, 2 * tile)
    attend(p1_ref, y1_ref, 0, 2 * s - 1, spread([project_job(lo, p0_ref, j) for j in range(n_jobs)]))
    attend(p0_ref, y0_ref, 1, 2 * s,
           spread([both(project_job(hi, p1_ref, j), branch_job(y1_ref, lo, j)) for j in range(n_jobs)]))
    for j in range(n_jobs):
        branch_job(y0_ref, hi, j)()


def _attention_branch(h_bf16, w_in, w_branch, layer, sink, kv_pad, cos_pad, sin_pad):
    d = h_bf16.shape[1]
    tile = BRANCH_TILE
    halo_per_pair = 2 * tile // BLOCK
    n_halo = kv_pad.shape[0] // BLOCK
    kv_width = 2 * KV_WIDTH
    tab_spec = pl.BlockSpec((2 * tile, HEAD_DIM), lambda i: (i, 0))
    specs = [pl.BlockSpec(memory_space=pltpu.SMEM),
             pl.BlockSpec((2 * tile, kv_width), lambda i: (i, 0)),
             pl.BlockSpec((BLOCK, kv_width), lambda i: (jnp.maximum(i * halo_per_pair - 1, 0), 0)),
             pl.BlockSpec((BLOCK, kv_width), lambda i: (jnp.minimum((i + 1) * halo_per_pair, n_halo - 1), 0)),
             tab_spec, tab_spec]
    proj_chunks = _column_chunks((OFF_Q, ATTN_WIDTH), (OFF_ATTN_Z, ATTN_WIDTH))
    return _branch_call(_attn_branch_kernel, "attention_branch", h_bf16, w_in, proj_chunks, w_branch, layer,
                        [sink, kv_pad, kv_pad, kv_pad, cos_pad, sin_pad], specs, ATTN_WIDTH, d)


GATE_C_CHUNKS = _column_chunks((OFF_GATE_C, D_MODEL))
OUT_CHUNKS = _column_chunks((0, D_MODEL))


def _final_kernel(layer, n_out, w_in_hbm, w_out_hbm, cab0_ref, cab1_ref, cc0_ref, cc1_ref,
                  hb_ref, h_ref, gb_ref, g_ref, b_ref, *refs):
    out_refs = refs[:n_out]
    wr_ref, wo_ref, stage_ref, sem_ref = refs[n_out:]

    @pl.when(pl.program_id(0) == 0)
    def _():
        _load_weights(w_in_hbm, layer, GATE_C_CHUNKS, wr_ref, stage_ref, sem_ref)
        _load_weights(w_out_hbm, layer, OUT_CHUNKS, wo_ref, stage_ref, sem_ref)

    tile = BRANCH_TILE
    for t, (cab_ref, cc_ref) in enumerate(((cab0_ref, cc0_ref), (cab1_ref, cc1_ref))):
        rows = slice(t * tile, (t + 1) * tile)
        gate_c = _sigmoid(jnp.dot(hb_ref[rows, :], wr_ref[...], preferred_element_type=F32) + gb_ref[...])
        merged = cab_ref[...].astype(F32) + gate_c * cc_ref[...].astype(F32)
        out = jnp.dot(merged.astype(BF16), wo_ref[...], preferred_element_type=F32)
        y = _layer_norm(ALPHA * h_ref[rows, :] + out, g_ref[...], b_ref[...])
        out_refs[0][rows, :] = y
        if n_out > 1:
            out_refs[1][rows, :] = y.astype(BF16)


def _merge_project_norm(contrib_ab, contrib_c, h_bf16, h, w_in, w_out, layer, gate_bias_c, ln_g, ln_b, emit_bf16):
    s, d = h.shape
    tile = BRANCH_TILE
    row = pl.BlockSpec((2 * tile, d), lambda i: (i, 0))
    shifted = [pl.BlockSpec((tile, d), lambda i: (2 * i + 1, 0)), pl.BlockSpec((tile, d), lambda i: (2 * i + 2, 0))]
    out_specs = [row, row] if emit_bf16 else [row]
    out_shape = [jax.ShapeDtypeStruct((s, d), F32)] + ([jax.ShapeDtypeStruct((s, d), BF16)] if emit_bf16 else [])
    return pl.pallas_call(
        functools.partial(_final_kernel, layer, len(out_specs)),
        grid=(s // (2 * tile),),
        in_specs=[HBM, HBM] + shifted * 2 + [row, row, _resident((1, d)), _resident((1, d)), _resident((1, d))],
        out_specs=out_specs,
        out_shape=out_shape,
        scratch_shapes=[_weight_scratch(d, len(GATE_C_CHUNKS)), _weight_scratch(d, len(OUT_CHUNKS))]
        + _staging_scratch(d),
        compiler_params=_params("arbitrary"),
        name="merge_project_norm",
    )(w_in, w_out, contrib_ab, contrib_ab, contrib_c, contrib_c, h_bf16, h,
      gate_bias_c.reshape(1, d), ln_g.reshape(1, d), ln_b.reshape(1, d))


def kernel(x, positions, ln0_g, ln0_b, w_in, conv_w, gmlp_ln_g, gmlp_ln_b, spatial_w, spatial_b, sink,
           w_branch_a, w_branch_b, w_branch_c, gate_b, w_out, ln_g, ln_b):
    bsz, s, d = x.shape
    assert (bsz, d) == (1, D_MODEL) and s % 1024 == 0 and w_in.shape == (DEPTH, D_MODEL, IN_WIDTH)

    cos_pad, sin_pad = _rope_tables(jnp.pad(positions.reshape(s), (BRANCH_TILE, BRANCH_TILE)))
    h = h_bf16 = None
    for l in range(DEPTH):
        if l == 0:
            kv_pad, h, h_bf16 = _kv_projection(x.reshape(s, d), w_in, l, cos_pad, sin_pad, (ln0_g, ln0_b))
        else:
            kv_pad = _kv_projection(h_bf16, w_in, l, cos_pad, sin_pad)
        contrib_a = _conv_branch(h_bf16, w_in, w_branch_a, l, conv_w[l], gate_b[l, 0])
        contrib_ab = _gmlp_branch(h_bf16, w_in, w_branch_b, l, gmlp_ln_g[l], gmlp_ln_b[l],
                                  spatial_w[l], spatial_b[l], gate_b[l, 1], accumulate_onto=contrib_a)
        contrib_c = _attention_branch(h_bf16, w_in, w_branch_c, l, sink[l], kv_pad, cos_pad, sin_pad)
        outs = _merge_project_norm(contrib_ab, contrib_c, h_bf16, h, w_in, w_out, l, gate_b[l, 2],
                                   ln_g[l], ln_b[l], emit_bf16=l + 1 < DEPTH)
        h = outs[0]
        h_bf16 = outs[1] if l + 1 < DEPTH else None
    return h.reshape(bsz, s, d)
```

```python
import functools

import jax
import jax.numpy as jnp
from jax import lax
from jax.experimental import pallas as pl
from jax.experimental.pallas import tpu as pltpu

D_MODEL = 2048
DEPTH = 2
HEAD_DIM = 128
N_Q_HEADS = 16
N_KV_HEADS = 4
GROUP = N_Q_HEADS // N_KV_HEADS
ATTN_WIDTH = N_Q_HEADS * HEAD_DIM
KV_WIDTH = N_KV_HEADS * HEAD_DIM
CONV_WIDTH = 1024
GMLP_WIDTH = 1024
GMLP_GROUPS = 8
GMLP_GROUP_DIM = GMLP_WIDTH // GMLP_GROUPS
CHUNK = 128
BLOCK = 128
ROPE_THETA = 500000.0
ROPE_DIM = HEAD_DIM // 4
ROPE_HALF = ROPE_DIM // 2
LN_EPS = 1e-5
LOG2_E = 1.4426950408889634
ALPHA = (2.0 * DEPTH) ** 0.25
IN_WIDTH = 4 * CONV_WIDTH + 3 * GMLP_WIDTH + 2 * ATTN_WIDTH + 2 * KV_WIDTH + 3 * D_MODEL

OFF_CONV, OFF_GMLP, OFF_Q, OFF_KV, OFF_ATTN_Z = 0, 4096, 7168, 9216, 10240
OFF_GATE_A, OFF_GATE_B, OFF_GATE_C = 12288, 14336, 16384

PROJ_CHUNK = 1024
BRANCH_TILE = 256
UNITS_PER_JOB = 2
VMEM_LIMIT_BYTES = 60 * 1024 * 1024

F32 = jnp.float32
BF16 = jnp.bfloat16


def _params(*semantics):
    return pltpu.CompilerParams(dimension_semantics=semantics, vmem_limit_bytes=VMEM_LIMIT_BYTES)


def _silu(x):
    return x / (1.0 + jnp.exp(-x))


def _sigmoid(x):
    return 1.0 / (1.0 + jnp.exp(-x))


def _gelu_tanh(x):
    return 0.5 * x * (1.0 + jnp.tanh(0.7978845608028654 * (x + 0.044715 * (x * x * x))))


def _layer_norm(x, g, b):
    mu = jnp.mean(x, axis=-1, keepdims=True)
    xc = x - mu
    var = jnp.mean(xc * xc, axis=-1, keepdims=True)
    return xc * lax.rsqrt(var + LN_EPS) * g + b


def _resident(shape):
    return pl.BlockSpec(shape, lambda i: (0,) * len(shape), pipeline_mode=pl.Buffered(1))


def _rope_table_kernel(pos_ref, invf_ref, cos_ref, sin_ref):
    ang = pos_ref[...].astype(F32) * invf_ref[...]
    lane = lax.broadcasted_iota(jnp.int32, ang.shape, 1)
    c = jnp.cos(ang)
    s = jnp.sin(ang)
    cos_ref[...] = jnp.where(lane < ROPE_DIM, c, 1.0)
    sin_ref[...] = jnp.where(lane < ROPE_HALF, -s, jnp.where(lane < ROPE_DIM, s, 0.0))


def _rope_tables(positions, rows=512):
    s = positions.shape[0]
    inv_freq = ROPE_THETA ** (-jnp.arange(ROPE_HALF, dtype=F32) / ROPE_HALF)
    invf = jnp.tile(inv_freq, HEAD_DIM // ROPE_HALF).reshape(1, HEAD_DIM)
    tab = jax.ShapeDtypeStruct((s, HEAD_DIM), F32)
    tab_spec = pl.BlockSpec((rows, HEAD_DIM), lambda i: (i, 0))
    return pl.pallas_call(
        _rope_table_kernel,
        grid=(s // rows,),
        in_specs=[pl.BlockSpec((rows, 1), lambda i: (i, 0)), pl.BlockSpec((1, HEAD_DIM), lambda i: (0, 0))],
        out_specs=[tab_spec, tab_spec],
        out_shape=[tab, tab],
        compiler_params=_params("parallel"),
        name="rope_tables",
    )(positions.reshape(s, 1), invf)


def _rope(t, cos, sin):
    lane = lax.broadcasted_iota(jnp.int32, t.shape, 1)
    partner = jnp.where(lane < ROPE_HALF, pltpu.roll(t, HEAD_DIM - ROPE_HALF, 1), pltpu.roll(t, ROPE_HALF, 1))
    return t * cos + partner * sin


WEIGHT_ROWS = 64
MAX_SEGMENT_WIDTH = 4096
HBM = pl.BlockSpec(memory_space=pl.ANY)


def _column_chunks(*segments):
    return tuple(segments)


def _width(segments):
    return sum(width for _, width in segments)


def _weight_scratch(k, segments):
    return pltpu.VMEM((k, _width(segments)), BF16)


def _staging_scratch():
    return [pltpu.VMEM((2, WEIGHT_ROWS, MAX_SEGMENT_WIDTH), F32), pltpu.SemaphoreType.DMA((2,))]


def _load_weights(src_hbm, layer, segments, dst_ref, stage_ref, sem_ref):
    k = dst_ref.shape[0]
    chunks = []
    dst_col = 0
    for off, width in segments:
        assert width <= MAX_SEGMENT_WIDTH and k % WEIGHT_ROWS == 0
        chunks += [(r, off, width, dst_col) for r in range(0, k, WEIGHT_ROWS)]
        dst_col += width

    def copy(i):
        r, off, width, _ = chunks[i]
        slot = i % 2
        return pltpu.make_async_copy(src_hbm.at[layer, pl.ds(r, WEIGHT_ROWS), pl.ds(off, width)],
                                     stage_ref.at[slot, :, pl.ds(0, width)], sem_ref.at[slot])

    copy(0).start()
    for i, (r, _, width, dst_col) in enumerate(chunks):
        if i + 1 < len(chunks):
            copy(i + 1).start()
        copy(i).wait()
        dst_ref[r:r + WEIGHT_ROWS, dst_col:dst_col + width] = stage_ref[i % 2, :, :width].astype(BF16)


def _project(h_ref, rows, w_ref, p_ref):
    h = h_ref[rows, :]
    for c in range(0, p_ref.shape[1], PROJ_CHUNK):
        p_ref[:, c:c + PROJ_CHUNK] = jnp.dot(
            h, w_ref[:, c:c + PROJ_CHUNK], preferred_element_type=F32).astype(BF16)


def _branch_call(body, name, h_bf16, w_in, proj_chunks, w_branch, layer, extra_inputs, extra_specs, y_width,
                 out_width):
    s, d = h_bf16.shape
    tile = BRANCH_TILE
    n_pairs = s // (2 * tile)
    proj_width = _width(proj_chunks)
    branch_k, branch_n = w_branch.shape[1:]
    branch_chunks = _column_chunks((0, branch_n))
    n_extra = len(extra_inputs)

    def kernel_fn(w_in_hbm, w_branch_hbm, h_ref, *rest):
        extra_refs, o_ref = rest[:n_extra], rest[n_extra]
        w_ref, wb_ref, stage_ref, sem_ref, p0_ref, p1_ref, y0_ref, y1_ref = rest[n_extra + 1:]

        @pl.when(pl.program_id(0) == 0)
        def _():
            _load_weights(w_in_hbm, layer, proj_chunks, w_ref, stage_ref, sem_ref)
            _load_weights(w_branch_hbm, layer, branch_chunks, wb_ref, stage_ref, sem_ref)

        body(h_ref, w_ref, wb_ref, *extra_refs, o_ref, p0_ref, p1_ref, y0_ref, y1_ref)

    return pl.pallas_call(
        kernel_fn,
        grid=(n_pairs + 1,),
        in_specs=[HBM, HBM, pl.BlockSpec((2 * tile, d), lambda i: (jnp.minimum(i, n_pairs - 1), 0))] + extra_specs,
        out_specs=pl.BlockSpec((2 * tile, out_width), lambda i: (i, 0)),
        out_shape=jax.ShapeDtypeStruct(((n_pairs + 1) * 2 * tile, out_width), BF16),
        scratch_shapes=[_weight_scratch(d, proj_chunks), _weight_scratch(branch_k, branch_chunks)]
        + _staging_scratch()
        + [pltpu.VMEM((tile, proj_width), BF16), pltpu.VMEM((tile, proj_width), BF16),
           pltpu.VMEM((tile, y_width), BF16), pltpu.VMEM((tile, y_width), BF16)],
        compiler_params=_params("arbitrary"),
        name=name,
    )(w_in, w_branch, h_bf16, *extra_inputs)


def _conv_branch_kernel(h_ref, w_ref, wa_ref, cw_ref, gb_ref, o_ref, p0_ref, p1_ref, y0_ref, y1_ref):
    tile = BRANCH_TILE
    s = pl.program_id(0)
    n_tiles = 2 * (pl.num_programs(0) - 1)
    cw = cw_ref[...]

    @pl.when(s == 0)
    def _():
        p0_ref[...] = jnp.zeros_like(p0_ref)
        p1_ref[...] = jnp.zeros_like(p1_ref)

    def gated_input_row(p_ref, row):
        c = p_ref[row:row + 1, CONV_WIDTH:2 * CONV_WIDTH].astype(F32)
        return c * p_ref[row:row + 1, 2 * CONV_WIDTH:3 * CONV_WIDTH].astype(F32)

    def mix(p_ref, y_ref, rows, tile_idx, y_prev, y_next):
        y = p_ref[:, CONV_WIDTH:2 * CONV_WIDTH].astype(F32) * p_ref[:, 2 * CONV_WIDTH:3 * CONV_WIDTH].astype(F32)
        y_prev = jnp.where(tile_idx > 0, y_prev, 0.0)
        y_next = jnp.where(tile_idx < n_tiles - 1, y_next, 0.0)
        row = lax.broadcasted_iota(jnp.int32, y.shape, 0)
        up = jnp.where(row == 0, y_prev, pltpu.roll(y, 1, 0))
        dn = jnp.where(row == tile - 1, y_next, pltpu.roll(y, tile - 1, 0))
        conv = cw[0:1, :] * up + cw[1:2, :] * y + cw[2:3, :] * dn
        z = p_ref[:, 3 * CONV_WIDTH:4 * CONV_WIDTH].astype(F32)
        y_ref[...] = (p_ref[:, :CONV_WIDTH].astype(F32) * conv * _silu(z)).astype(BF16)
        gate = _sigmoid(p_ref[:, 4 * CONV_WIDTH:].astype(F32) + gb_ref[...])
        o_ref[rows, :] = (gate * jnp.dot(y_ref[...], wa_ref[...], preferred_element_type=F32)).astype(BF16)

    lo, hi = slice(0, tile), slice(tile, 2 * tile)
    last_of_tile_before_p1 = gated_input_row(p0_ref, tile - 1)
    _project(h_ref, lo, w_ref, p0_ref)
    last_of_p1 = gated_input_row(p1_ref, tile - 1)
    mix(p1_ref, y1_ref, lo, 2 * s - 1, last_of_tile_before_p1, gated_input_row(p0_ref, 0))
    _project(h_ref, hi, w_ref, p1_ref)
    mix(p0_ref, y0_ref, hi, 2 * s, last_of_p1, gated_input_row(p1_ref, 0))


def _conv_branch(h_bf16, w_in, w_branch, layer, conv_w, gate_bias):
    d = h_bf16.shape[1]
    proj_chunks = _column_chunks((OFF_CONV, 4 * CONV_WIDTH), (OFF_GATE_A, d))
    return _branch_call(_conv_branch_kernel, "conv_branch", h_bf16, w_in, proj_chunks, w_branch, layer,
                        [conv_w, gate_bias.reshape(1, d)], [_resident((3, CONV_WIDTH)), _resident((1, d))],
                        CONV_WIDTH, d)


def _gmlp_branch_kernel(h_ref, w_ref, wb_ref, g_ref, b_ref, ws_ref, bias_ref, gb_ref, acc_ref, o_ref,
                        p0_ref, p1_ref, y0_ref, y1_ref):
    tile = BRANCH_TILE

    @pl.when(pl.program_id(0) == 0)
    def _():
        p1_ref[...] = jnp.zeros_like(p1_ref)

    proj_width = p0_ref.shape[1]
    n_jobs = 2 + tile // CHUNK * GMLP_GROUPS // 2
    job_cols = proj_width // n_jobs

    def mix(p_ref, y_ref, rows, h_rows, q_ref):
        jobs = [slice(u * job_cols, (u + 1) * job_cols) for u in range(n_jobs)]

        def run_job():
            c = jobs.pop(0)
            q_ref[:, c] = jnp.dot(h_ref[h_rows, :], w_ref[:, c], preferred_element_type=F32).astype(BF16)

        run_job()
        run_job()
        v = _layer_norm(_gelu_tanh(p_ref[:, GMLP_WIDTH:2 * GMLP_WIDTH].astype(F32)), g_ref[...], b_ref[...])
        v = v.astype(BF16)
        for c in range(tile // CHUNK):
            r = slice(c * CHUNK, (c + 1) * CHUNK)
            for g in range(GMLP_GROUPS):
                if g % 2 == 0:
                    run_job()
                cols = slice(g * GMLP_GROUP_DIM, (g + 1) * GMLP_GROUP_DIM)
                zcols = slice(2 * GMLP_WIDTH + g * GMLP_GROUP_DIM, 2 * GMLP_WIDTH + (g + 1) * GMLP_GROUP_DIM)
                mixed = jnp.dot(ws_ref[g], v[r, cols], preferred_element_type=F32) + bias_ref[:, cols]
                u = _gelu_tanh(p_ref[r, cols].astype(F32))
                y_ref[r, cols] = (u * mixed * _silu(p_ref[r, zcols].astype(F32))).astype(BF16)
        assert not jobs
        gate = _sigmoid(p_ref[:, 3 * GMLP_WIDTH:].astype(F32) + gb_ref[...])
        contrib = gate * jnp.dot(y_ref[...], wb_ref[...], preferred_element_type=F32)
        o_ref[rows, :] = (acc_ref[rows, :].astype(F32) + contrib).astype(BF16)

    lo, hi = slice(0, tile), slice(tile, 2 * tile)
    mix(p1_ref, y1_ref, lo, lo, p0_ref)
    mix(p0_ref, y0_ref, hi, hi, p1_ref)


def _gmlp_branch(h_bf16, w_in, w_branch, layer, ln_g, ln_b, spatial_w, spatial_b, gate_bias, accumulate_onto):
    d = h_bf16.shape[1]
    bias = jnp.repeat(spatial_b.T, GMLP_GROUP_DIM, axis=1)
    proj_chunks = _column_chunks((OFF_GMLP, 3 * GMLP_WIDTH), (OFF_GATE_B, d))
    return _branch_call(
        _gmlp_branch_kernel, "gmlp_branch", h_bf16, w_in, proj_chunks, w_branch, layer,
        [ln_g.reshape(1, -1), ln_b.reshape(1, -1), spatial_w.astype(BF16), bias, gate_bias.reshape(1, d),
         accumulate_onto],
        [_resident((1, GMLP_WIDTH)), _resident((1, GMLP_WIDTH)), _resident((GMLP_GROUPS, CHUNK, CHUNK)),
         _resident((CHUNK, GMLP_WIDTH)), _resident((1, d)),
         pl.BlockSpec((2 * BRANCH_TILE, d), lambda i: (i, 0))],
        GMLP_WIDTH, d)


KV_CHUNKS = _column_chunks((OFF_KV, 2 * KV_WIDTH))


def _kv_kernel(layer, w_in_hbm, h_ref, cos_ref, sin_ref, o_ref, w_ref, stage_ref, sem_ref):
    _kv_body(layer, w_in_hbm, h_ref[...], cos_ref, sin_ref, o_ref, w_ref, stage_ref, sem_ref)


def _norm_kv_kernel(layer, w_in_hbm, x_ref, g_ref, b_ref, cos_ref, sin_ref, o_ref, hf_ref, hb_ref,
                    w_ref, stage_ref, sem_ref):
    h = _layer_norm(x_ref[...], g_ref[...], b_ref[...])
    hf_ref[...] = h
    hb = h.astype(BF16)
    hb_ref[...] = hb
    _kv_body(layer, w_in_hbm, hb, cos_ref, sin_ref, o_ref, w_ref, stage_ref, sem_ref)


def _kv_body(layer, w_in_hbm, h, cos_ref, sin_ref, o_ref, w_ref, stage_ref, sem_ref):
    i = pl.program_id(0)

    @pl.when(i == 0)
    def _():
        _load_weights(w_in_hbm, layer, KV_CHUNKS, w_ref, stage_ref, sem_ref)

    kv = jnp.dot(h, w_ref[...], preferred_element_type=F32)
    cos = cos_ref[...]
    sin = sin_ref[...]
    parts = [_rope(kv[:, g * HEAD_DIM:(g + 1) * HEAD_DIM], cos, sin) for g in range(N_KV_HEADS)]
    out = jnp.concatenate(parts + [kv[:, KV_WIDTH:]], axis=1)
    inside = (i > 0) & (i < pl.num_programs(0) - 1)
    o_ref[...] = jnp.where(inside, out, 0.0).astype(BF16)


def _kv_projection(h_or_x, w_in, layer, cos_pad, sin_pad, input_norm=None):
    s, d = h_or_x.shape
    tile = BRANCH_TILE
    n_tiles = s // tile
    tab_spec = pl.BlockSpec((tile, HEAD_DIM), lambda i: (i, 0))
    row_spec = pl.BlockSpec((tile, d), lambda i: (jnp.clip(i - 1, 0, n_tiles - 1), 0))
    kv_spec = pl.BlockSpec((tile, 2 * KV_WIDTH), lambda i: (i, 0))
    kv_shape = jax.ShapeDtypeStruct(((n_tiles + 2) * tile, 2 * KV_WIDTH), BF16)
    scratch = [_weight_scratch(d, KV_CHUNKS)] + _staging_scratch()
    if input_norm is None:
        return pl.pallas_call(
            functools.partial(_kv_kernel, layer),
            grid=(n_tiles + 2,),
            in_specs=[HBM, row_spec, tab_spec, tab_spec],
            out_specs=kv_spec,
            out_shape=kv_shape,
            scratch_shapes=scratch,
            compiler_params=_params("arbitrary"),
            name="kv_projection",
        )(w_in, h_or_x, cos_pad, sin_pad)
    g, b = input_norm
    return pl.pallas_call(
        functools.partial(_norm_kv_kernel, layer),
        grid=(n_tiles + 2,),
        in_specs=[HBM, row_spec, _resident((1, d)), _resident((1, d)), tab_spec, tab_spec],
        out_specs=[kv_spec, row_spec, row_spec],
        out_shape=[kv_shape, jax.ShapeDtypeStruct((s, d), F32), jax.ShapeDtypeStruct((s, d), BF16)],
        scratch_shapes=scratch,
        compiler_params=_params("arbitrary"),
        name="norm_kv_projection",
    )(w_in, h_or_x, g.reshape(1, d), b.reshape(1, d), cos_pad, sin_pad)


def _attn_branch_kernel(h_ref, w_ref, wc_ref, sink_ref, kv_ref, kvp_ref, kvn_ref, cos_ref, sin_ref, o_ref,
                        p0_ref, p1_ref, y0_ref, y1_ref):
    tile = BRANCH_TILE
    n_blocks = tile // BLOCK
    s = pl.program_id(0)
    n_tiles = 2 * (pl.num_programs(0) - 1)
    scale = HEAD_DIM ** -0.5 * LOG2_E

    @pl.when(s == 0)
    def _():
        p1_ref[...] = jnp.zeros_like(p1_ref)

    q_row = lax.broadcasted_iota(jnp.int32, (GROUP * BLOCK, 3 * BLOCK), 0) % BLOCK
    k_col = lax.broadcasted_iota(jnp.int32, (GROUP * BLOCK, 3 * BLOCK), 1)
    band = (k_col >= q_row) & (k_col <= q_row + 2 * BLOCK)
    head_of_row = lax.broadcasted_iota(jnp.int32, (GROUP * BLOCK, 1), 0) // BLOCK

    def band_rows(w, b, cols):
        first = w * n_blocks + b
        if first == 0:
            return jnp.concatenate([kvp_ref[:, cols], kv_ref[:2 * BLOCK, cols]], axis=0)
        if first == 2 * n_blocks - 1:
            return jnp.concatenate([kv_ref[(first - 1) * BLOCK:, cols], kvn_ref[:, cols]], axis=0)
        return kv_ref[(first - 1) * BLOCK:(first + 2) * BLOCK, cols]

    units = [(g, b) for g in range(N_KV_HEADS) for b in range(n_blocks)]
    n_jobs = len(units) // UNITS_PER_JOB
    job_cols = 2 * ATTN_WIDTH // n_jobs
    out_cols = D_MODEL // n_jobs

    def attend(p_ref, y_ref, w, tile_idx, mxu_jobs):
        rows = slice(w * tile, (w + 1) * tile)
        cos_q = cos_ref[rows, :]
        sin_q = sin_ref[rows, :]
        q_heads = {}

        def scores_of(g, b):
            if g not in q_heads:
                q_heads[g] = []
                for j in range(GROUP):
                    c0 = (g * GROUP + j) * HEAD_DIM
                    q = p_ref[:, c0:c0 + HEAD_DIM].astype(F32)
                    q_heads[g].append((_rope(q, cos_q, sin_q) * scale).astype(BF16))
            kb = band_rows(w, b, slice(g * HEAD_DIM, (g + 1) * HEAD_DIM))
            q4 = jnp.concatenate([qh[b * BLOCK:(b + 1) * BLOCK] for qh in q_heads[g]], axis=0)
            return lax.dot_general(q4, kb, (((1,), (1,)), ((), ())), preferred_element_type=F32)

        scores = scores_of(*units[0])
        for u, (g, b) in enumerate(units):
            mxu_jobs[u]()
            r = slice(b * BLOCK, (b + 1) * BLOCK)
            sink_col = jnp.zeros((GROUP * BLOCK, 1), F32)
            for j in range(GROUP):
                sink_col = jnp.where(head_of_row == j, sink_ref[g * GROUP + j] * LOG2_E, sink_col)
            valid = band
            if b == 0:
                valid = valid & (k_col >= jnp.where(tile_idx == 0, BLOCK, 0))
            if b == n_blocks - 1:
                valid = valid & (k_col < jnp.where(tile_idx == n_tiles - 1, 2 * BLOCK, 3 * BLOCK))
            scores = jnp.where(valid, scores, -jnp.inf)
            m = jnp.maximum(jnp.max(scores, axis=-1, keepdims=True), sink_col)
            p = jnp.exp2(scores - m)
            denom = jnp.sum(p, axis=-1, keepdims=True) + jnp.exp2(sink_col - m)
            if u + 1 < len(units):
                scores = scores_of(*units[u + 1])
            vb = band_rows(w, b, slice(KV_WIDTH + g * HEAD_DIM, KV_WIDTH + (g + 1) * HEAD_DIM))
            out = jnp.dot(p.astype(BF16), vb, preferred_element_type=F32) / denom
            for j in range(GROUP):
                c0 = (g * GROUP + j) * HEAD_DIM
                zj = p_ref[r, ATTN_WIDTH + c0:ATTN_WIDTH + c0 + HEAD_DIM].astype(F32)
                y_ref[r, c0:c0 + HEAD_DIM] = (out[j * BLOCK:(j + 1) * BLOCK] * _silu(zj)).astype(BF16)

    def project_job(h_rows, p_ref, u):
        def job():
            c = slice(u * job_cols, (u + 1) * job_cols)
            p_ref[:, c] = jnp.dot(h_ref[h_rows, :], w_ref[:, c], preferred_element_type=F32).astype(BF16)
        return job

    def branch_job(y_ref, out_rows, u):
        def job():
            c = slice(u * out_cols, (u + 1) * out_cols)
            o_ref[out_rows, c] = jnp.dot(y_ref[...], wc_ref[:, c], preferred_element_type=F32).astype(BF16)
        return job

    def both(*jobs):
        def job():
            for j in jobs:
                j()
        return job

    def spread(jobs):
        return [jobs[u // UNITS_PER_JOB] if u % UNITS_PER_JOB == 0 else both() for u in range(len(units))]

    lo, hi = slice(0, tile), slice(tile, 2 * tile)
    attend(p1_ref, y1_ref, 0, 2 * s - 1, spread([project_job(lo, p0_ref, j) for j in range(n_jobs)]))
    attend(p0_ref, y0_ref, 1, 2 * s,
           spread([both(project_job(hi, p1_ref, j), branch_job(y1_ref, lo, j)) for j in range(n_jobs)]))
    for j in range(n_jobs):
        branch_job(y0_ref, hi, j)()


def _attention_branch(h_bf16, w_in, w_branch, layer, sink, kv_pad, cos_pad, sin_pad):
    d = h_bf16.shape[1]
    tile = BRANCH_TILE
    halo_per_pair = 2 * tile // BLOCK
    n_halo = kv_pad.shape[0] // BLOCK
    kv_width = 2 * KV_WIDTH
    tab_spec = pl.BlockSpec((2 * tile, HEAD_DIM), lambda i: (i, 0))
    specs = [pl.BlockSpec(memory_space=pltpu.SMEM),
             pl.BlockSpec((2 * tile, kv_width), lambda i: (i, 0)),
             pl.BlockSpec((BLOCK, kv_width), lambda i: (jnp.maximum(i * halo_per_pair - 1, 0), 0)),
             pl.BlockSpec((BLOCK, kv_width), lambda i: (jnp.minimum((i + 1) * halo_per_pair, n_halo - 1), 0)),
             tab_spec, tab_spec]
    proj_chunks = _column_chunks((OFF_Q, ATTN_WIDTH), (OFF_ATTN_Z, ATTN_WIDTH))
    return _branch_call(_attn_branch_kernel, "attention_branch", h_bf16, w_in, proj_chunks, w_branch, layer,
                        [sink, kv_pad, kv_pad, kv_pad, cos_pad, sin_pad], specs, ATTN_WIDTH, d)


GATE_C_CHUNKS = _column_chunks((OFF_GATE_C, D_MODEL))
OUT_CHUNKS = _column_chunks((0, D_MODEL))


def _final_kernel(layer, n_out, w_in_hbm, w_out_hbm, cab0_ref, cab1_ref, cc0_ref, cc1_ref,
                  hb_ref, h_ref, gb_ref, g_ref, b_ref, *refs):
    out_refs = refs[:n_out]
    wr_ref, wo_ref, stage_ref, sem_ref = refs[n_out:]

    @pl.when(pl.program_id(0) == 0)
    def _():
        _load_weights(w_in_hbm, layer, GATE_C_CHUNKS, wr_ref, stage_ref, sem_ref)
        _load_weights(w_out_hbm, layer, OUT_CHUNKS, wo_ref, stage_ref, sem_ref)

    tile = BRANCH_TILE
    for t, (cab_ref, cc_ref) in enumerate(((cab0_ref, cc0_ref), (cab1_ref, cc1_ref))):
        rows = slice(t * tile, (t + 1) * tile)
        gate_c = _sigmoid(jnp.dot(hb_ref[rows, :], wr_ref[...], preferred_element_type=F32) + gb_ref[...])
        merged = cab_ref[...].astype(F32) + gate_c * cc_ref[...].astype(F32)
        out = jnp.dot(merged.astype(BF16), wo_ref[...], preferred_element_type=F32)
        y = _layer_norm(ALPHA * h_ref[rows, :] + out, g_ref[...], b_ref[...])
        out_refs[0][rows, :] = y
        if n_out > 1:
            out_refs[1][rows, :] = y.astype(BF16)


def _merge_project_norm(contrib_ab, contrib_c, h_bf16, h, w_in, w_out, layer, gate_bias_c, ln_g, ln_b, emit_bf16):
    s, d = h.shape
    tile = BRANCH_TILE
    row = pl.BlockSpec((2 * tile, d), lambda i: (i, 0))
    shifted = [pl.BlockSpec((tile, d), lambda i: (2 * i + 1, 0)), pl.BlockSpec((tile, d), lambda i: (2 * i + 2, 0))]
    out_specs = [row, row] if emit_bf16 else [row]
    out_shape = [jax.ShapeDtypeStruct((s, d), F32)] + ([jax.ShapeDtypeStruct((s, d), BF16)] if emit_bf16 else [])
    return pl.pallas_call(
        functools.partial(_final_kernel, layer, len(out_specs)),
        grid=(s // (2 * tile),),
        in_specs=[HBM, HBM] + shifted * 2 + [row, row, _resident((1, d)), _resident((1, d)), _resident((1, d))],
        out_specs=out_specs,
        out_shape=out_shape,
        scratch_shapes=[_weight_scratch(d, GATE_C_CHUNKS), _weight_scratch(d, OUT_CHUNKS)] + _staging_scratch(),
        compiler_params=_params("arbitrary"),
        name="merge_project_norm",
    )(w_in, w_out, contrib_ab, contrib_ab, contrib_c, contrib_c, h_bf16, h,
      gate_bias_c.reshape(1, d), ln_g.reshape(1, d), ln_b.reshape(1, d))


def kernel(x, positions, ln0_g, ln0_b, w_in, conv_w, gmlp_ln_g, gmlp_ln_b, spatial_w, spatial_b, sink,
           w_branch_a, w_branch_b, w_branch_c, gate_b, w_out, ln_g, ln_b):
    bsz, s, d = x.shape
    assert (bsz, d) == (1, D_MODEL) and s % 1024 == 0 and w_in.shape == (DEPTH, D_MODEL, IN_WIDTH)

    cos_pad, sin_pad = _rope_tables(jnp.pad(positions.reshape(s), (BRANCH_TILE, BRANCH_TILE)))
    h = h_bf16 = None
    for l in range(DEPTH):
        if l == 0:
            kv_pad, h, h_bf16 = _kv_projection(x.reshape(s, d), w_in, l, cos_pad, sin_pad, (ln0_g, ln0_b))
        else:
            kv_pad = _kv_projection(h_bf16, w_in, l, cos_pad, sin_pad)
        contrib_a = _conv_branch(h_bf16, w_in, w_branch_a, l, conv_w[l], gate_b[l, 0])
        contrib_ab = _gmlp_branch(h_bf16, w_in, w_branch_b, l, gmlp_ln_g[l], gmlp_ln_b[l],
                                  spatial_w[l], spatial_b[l], gate_b[l, 1], accumulate_onto=contrib_a)
        contrib_c = _attention_branch(h_bf16, w_in, w_branch_c, l, sink[l], kv_pad, cos_pad, sin_pad)
        outs = _merge_project_norm(contrib_ab, contrib_c, h_bf16, h, w_in, w_out, l, gate_b[l, 2],
                                   ln_g[l], ln_b[l], emit_bf16=l + 1 < DEPTH)
        h = outs[0]
        h_bf16 = outs[1] if l + 1 < DEPTH else None
    return h.reshape(bsz, s, d)
```

```python
import jax
import jax.numpy as jnp
from jax import lax
from jax.experimental import pallas as pl
from jax.experimental.pallas import tpu as pltpu

D_MODEL = 2048
DEPTH = 2
HEAD_DIM = 128
N_Q_HEADS = 16
N_KV_HEADS = 4
GROUP = N_Q_HEADS // N_KV_HEADS
ATTN_WIDTH = N_Q_HEADS * HEAD_DIM
KV_WIDTH = N_KV_HEADS * HEAD_DIM
CONV_WIDTH = 1024
GMLP_WIDTH = 1024
GMLP_GROUPS = 8
GMLP_GROUP_DIM = GMLP_WIDTH // GMLP_GROUPS
CHUNK = 128
BLOCK = 128
ROPE_THETA = 500000.0
ROPE_DIM = HEAD_DIM // 4
ROPE_HALF = ROPE_DIM // 2
LN_EPS = 1e-5
LOG2_E = 1.4426950408889634
ALPHA = (2.0 * DEPTH) ** 0.25
IN_WIDTH = 4 * CONV_WIDTH + 3 * GMLP_WIDTH + 2 * ATTN_WIDTH + 2 * KV_WIDTH + 3 * D_MODEL

OFF_CONV, OFF_GMLP, OFF_Q, OFF_KV, OFF_ATTN_Z = 0, 4096, 7168, 9216, 10240
OFF_GATE_A, OFF_GATE_B, OFF_GATE_C = 12288, 14336, 16384

WEIGHT_TILE = 1024
PROJ_CHUNK = 1024
BRANCH_TILE = 256
UNITS_PER_JOB = 2
VMEM_LIMIT_BYTES = 56 * 1024 * 1024

F32 = jnp.float32
BF16 = jnp.bfloat16


def _params(*semantics):
    return pltpu.CompilerParams(dimension_semantics=semantics, vmem_limit_bytes=VMEM_LIMIT_BYTES)


def _silu(x):
    return x / (1.0 + jnp.exp(-x))


def _sigmoid(x):
    return 1.0 / (1.0 + jnp.exp(-x))


def _gelu_tanh(x):
    return 0.5 * x * (1.0 + jnp.tanh(0.7978845608028654 * (x + 0.044715 * (x * x * x))))


def _layer_norm(x, g, b):
    mu = jnp.mean(x, axis=-1, keepdims=True)
    xc = x - mu
    var = jnp.mean(xc * xc, axis=-1, keepdims=True)
    return xc * lax.rsqrt(var + LN_EPS) * g + b


def _resident(shape):
    return pl.BlockSpec(shape, lambda i: (0,) * len(shape), pipeline_mode=pl.Buffered(1))


def _layer_resident(layer, shape):
    return pl.BlockSpec((None,) + shape, lambda i: (layer,) + (0,) * len(shape), pipeline_mode=pl.Buffered(1))


def _rope_table_kernel(pos_ref, invf_ref, cos_ref, sin_ref):
    ang = pos_ref[...].astype(F32) * invf_ref[...]
    lane = lax.broadcasted_iota(jnp.int32, ang.shape, 1)
    c = jnp.cos(ang)
    s = jnp.sin(ang)
    cos_ref[...] = jnp.where(lane < ROPE_DIM, c, 1.0)
    sin_ref[...] = jnp.where(lane < ROPE_HALF, -s, jnp.where(lane < ROPE_DIM, s, 0.0))


def _rope_tables(positions, rows=512):
    s = positions.shape[0]
    inv_freq = ROPE_THETA ** (-jnp.arange(ROPE_HALF, dtype=F32) / ROPE_HALF)
    invf = jnp.tile(inv_freq, HEAD_DIM // ROPE_HALF).reshape(1, HEAD_DIM)
    tab = jax.ShapeDtypeStruct((s, HEAD_DIM), F32)
    tab_spec = pl.BlockSpec((rows, HEAD_DIM), lambda i: (i, 0))
    return pl.pallas_call(
        _rope_table_kernel,
        grid=(s // rows,),
        in_specs=[pl.BlockSpec((rows, 1), lambda i: (i, 0)), pl.BlockSpec((1, HEAD_DIM), lambda i: (0, 0))],
        out_specs=[tab_spec, tab_spec],
        out_shape=[tab, tab],
        compiler_params=_params("parallel"),
        name="rope_tables",
    )(positions.reshape(s, 1), invf)


def _rope(t, cos, sin):
    lane = lax.broadcasted_iota(jnp.int32, t.shape, 1)
    partner = jnp.where(lane < ROPE_HALF, pltpu.roll(t, HEAD_DIM - ROPE_HALF, 1), pltpu.roll(t, ROPE_HALF, 1))
    return t * cos + partner * sin


def _cast_kernel(tiles_ref, w_ref, o_ref):
    del tiles_ref
    o_ref[...] = w_ref[...].astype(o_ref.dtype)


def _gather_cast_columns(w, col_tiles, tile=WEIGHT_TILE):
    layers, k, _ = w.shape
    table = jnp.asarray(col_tiles, jnp.int32)
    return pl.pallas_call(
        _cast_kernel,
        grid_spec=pltpu.PrefetchScalarGridSpec(
            num_scalar_prefetch=1,
            grid=(layers, len(col_tiles)),
            in_specs=[pl.BlockSpec((None, k, tile), lambda l, j, t: (l, 0, t[j]))],
            out_specs=pl.BlockSpec((None, k, tile), lambda l, j, t: (l, 0, j)),
        ),
        out_shape=jax.ShapeDtypeStruct((layers, k, len(col_tiles) * tile), BF16),
        compiler_params=_params("parallel", "parallel"),
        name="gather_cast_columns",
    )(table, w)


def _cast_bf16(w):
    return _gather_cast_columns(w, tuple(range(w.shape[-1] // WEIGHT_TILE)))


def _project(h_ref, rows, w_ref, p_ref):
    h = h_ref[rows, :]
    for c in range(0, p_ref.shape[1], PROJ_CHUNK):
        p_ref[:, c:c + PROJ_CHUNK] = jnp.dot(
            h, w_ref[:, c:c + PROJ_CHUNK], preferred_element_type=F32).astype(BF16)


def _branch_call(body, name, h_bf16, w_proj, w_branch, layer, extra_inputs, extra_specs, y_width, out_width):
    s, d = h_bf16.shape
    tile = BRANCH_TILE
    n_pairs = s // (2 * tile)
    proj_width = w_proj.shape[2]
    return pl.pallas_call(
        body,
        grid=(n_pairs + 1,),
        in_specs=[pl.BlockSpec((2 * tile, d), lambda i: (jnp.minimum(i, n_pairs - 1), 0)),
                  _layer_resident(layer, (d, proj_width)), _layer_resident(layer, w_branch.shape[1:])] + extra_specs,
        out_specs=pl.BlockSpec((2 * tile, out_width), lambda i: (i, 0)),
        out_shape=jax.ShapeDtypeStruct(((n_pairs + 1) * 2 * tile, out_width), BF16),
        scratch_shapes=[pltpu.VMEM((tile, proj_width), BF16), pltpu.VMEM((tile, proj_width), BF16),
                        pltpu.VMEM((tile, y_width), BF16), pltpu.VMEM((tile, y_width), BF16)],
        compiler_params=_params("arbitrary"),
        name=name,
    )(h_bf16, w_proj, w_branch, *extra_inputs)


def _conv_branch_kernel(h_ref, w_ref, wa_ref, cw_ref, gb_ref, o_ref, p0_ref, p1_ref, y0_ref, y1_ref):
    tile = BRANCH_TILE
    s = pl.program_id(0)
    n_tiles = 2 * (pl.num_programs(0) - 1)
    cw = cw_ref[...]

    @pl.when(s == 0)
    def _():
        p0_ref[...] = jnp.zeros_like(p0_ref)
        p1_ref[...] = jnp.zeros_like(p1_ref)

    def gated_input_row(p_ref, row):
        c = p_ref[row:row + 1, CONV_WIDTH:2 * CONV_WIDTH].astype(F32)
        return c * p_ref[row:row + 1, 2 * CONV_WIDTH:3 * CONV_WIDTH].astype(F32)

    def mix(p_ref, y_ref, rows, tile_idx, y_prev, y_next):
        y = p_ref[:, CONV_WIDTH:2 * CONV_WIDTH].astype(F32) * p_ref[:, 2 * CONV_WIDTH:3 * CONV_WIDTH].astype(F32)
        y_prev = jnp.where(tile_idx > 0, y_prev, 0.0)
        y_next = jnp.where(tile_idx < n_tiles - 1, y_next, 0.0)
        row = lax.broadcasted_iota(jnp.int32, y.shape, 0)
        up = jnp.where(row == 0, y_prev, pltpu.roll(y, 1, 0))
        dn = jnp.where(row == tile - 1, y_next, pltpu.roll(y, tile - 1, 0))
        conv = cw[0:1, :] * up + cw[1:2, :] * y + cw[2:3, :] * dn
        z = p_ref[:, 3 * CONV_WIDTH:4 * CONV_WIDTH].astype(F32)
        y_ref[...] = (p_ref[:, :CONV_WIDTH].astype(F32) * conv * _silu(z)).astype(BF16)
        gate = _sigmoid(p_ref[:, 4 * CONV_WIDTH:].astype(F32) + gb_ref[...])
        o_ref[rows, :] = (gate * jnp.dot(y_ref[...], wa_ref[...], preferred_element_type=F32)).astype(BF16)

    lo, hi = slice(0, tile), slice(tile, 2 * tile)
    last_of_tile_before_p1 = gated_input_row(p0_ref, tile - 1)
    _project(h_ref, lo, w_ref, p0_ref)
    last_of_p1 = gated_input_row(p1_ref, tile - 1)
    mix(p1_ref, y1_ref, lo, 2 * s - 1, last_of_tile_before_p1, gated_input_row(p0_ref, 0))
    _project(h_ref, hi, w_ref, p1_ref)
    mix(p0_ref, y0_ref, hi, 2 * s, last_of_p1, gated_input_row(p1_ref, 0))


def _conv_branch(h_bf16, w_proj, w_branch, layer, conv_w, gate_bias):
    d = h_bf16.shape[1]
    return _branch_call(_conv_branch_kernel, "conv_branch", h_bf16, w_proj, w_branch, layer,
                        [conv_w, gate_bias.reshape(1, d)], [_resident((3, CONV_WIDTH)), _resident((1, d))],
                        CONV_WIDTH, d)


def _gmlp_branch_kernel(h_ref, w_ref, wb_ref, g_ref, b_ref, ws_ref, bias_ref, gb_ref, acc_ref, o_ref,
                        p0_ref, p1_ref, y0_ref, y1_ref):
    tile = BRANCH_TILE

    @pl.when(pl.program_id(0) == 0)
    def _():
        p1_ref[...] = jnp.zeros_like(p1_ref)

    proj_width = p0_ref.shape[1]
    n_jobs = 2 + GMLP_GROUPS
    job_cols = proj_width // n_jobs

    def mix(p_ref, y_ref, rows, h_rows, q_ref):
        jobs = [slice(u * job_cols, (u + 1) * job_cols) for u in range(n_jobs)]

        def run_job():
            c = jobs.pop(0)
            q_ref[:, c] = jnp.dot(h_ref[h_rows, :], w_ref[:, c], preferred_element_type=F32).astype(BF16)

        run_job()
        run_job()
        v = _layer_norm(_gelu_tanh(p_ref[:, GMLP_WIDTH:2 * GMLP_WIDTH].astype(F32)), g_ref[...], b_ref[...])
        v = v.astype(BF16)
        n_chunks = tile // CHUNK
        for g in range(GMLP_GROUPS):
            run_job()
            cols = slice(g * GMLP_GROUP_DIM, (g + 1) * GMLP_GROUP_DIM)
            zcols = slice(2 * GMLP_WIDTH + g * GMLP_GROUP_DIM, 2 * GMLP_WIDTH + (g + 1) * GMLP_GROUP_DIM)
            v_chunks = jnp.concatenate([v[c * CHUNK:(c + 1) * CHUNK, cols] for c in range(n_chunks)], axis=1)
            mixed_chunks = jnp.dot(ws_ref[g], v_chunks, preferred_element_type=F32)
            for c in range(n_chunks):
                r = slice(c * CHUNK, (c + 1) * CHUNK)
                mixed = mixed_chunks[:, c * GMLP_GROUP_DIM:(c + 1) * GMLP_GROUP_DIM] + bias_ref[:, cols]
                u = _gelu_tanh(p_ref[r, cols].astype(F32))
                y_ref[r, cols] = (u * mixed * _silu(p_ref[r, zcols].astype(F32))).astype(BF16)
        assert not jobs
        gate = _sigmoid(p_ref[:, 3 * GMLP_WIDTH:].astype(F32) + gb_ref[...])
        contrib = gate * jnp.dot(y_ref[...], wb_ref[...], preferred_element_type=F32)
        o_ref[rows, :] = (acc_ref[rows, :].astype(F32) + contrib).astype(BF16)

    lo, hi = slice(0, tile), slice(tile, 2 * tile)
    mix(p1_ref, y1_ref, lo, lo, p0_ref)
    mix(p0_ref, y0_ref, hi, hi, p1_ref)


def _gmlp_branch(h_bf16, w_proj, w_branch, layer, ln_g, ln_b, spatial_w, spatial_b, gate_bias, accumulate_onto):
    d = h_bf16.shape[1]
    bias = jnp.repeat(spatial_b.T, GMLP_GROUP_DIM, axis=1)
    return _branch_call(
        _gmlp_branch_kernel, "gmlp_branch", h_bf16, w_proj, w_branch, layer,
        [ln_g.reshape(1, -1), ln_b.reshape(1, -1), spatial_w.astype(BF16), bias, gate_bias.reshape(1, d),
         accumulate_onto],
        [_resident((1, GMLP_WIDTH)), _resident((1, GMLP_WIDTH)), _resident((GMLP_GROUPS, CHUNK, CHUNK)),
         _resident((CHUNK, GMLP_WIDTH)), _resident((1, d)),
         pl.BlockSpec((2 * BRANCH_TILE, d), lambda i: (i, 0))],
        GMLP_WIDTH, d)


def _kv_kernel(h_ref, w_ref, cos_ref, sin_ref, o_ref):
    _kv_body(h_ref[...], w_ref, cos_ref, sin_ref, o_ref)


def _norm_kv_kernel(x_ref, g_ref, b_ref, w_ref, cos_ref, sin_ref, o_ref, hf_ref, hb_ref):
    h = _layer_norm(x_ref[...], g_ref[...], b_ref[...])
    hf_ref[...] = h
    hb = h.astype(BF16)
    hb_ref[...] = hb
    _kv_body(hb, w_ref, cos_ref, sin_ref, o_ref)


def _kv_body(h, w_ref, cos_ref, sin_ref, o_ref):
    i = pl.program_id(0)
    kv = jnp.dot(h, w_ref[...], preferred_element_type=F32)
    cos = cos_ref[...]
    sin = sin_ref[...]
    parts = [_rope(kv[:, g * HEAD_DIM:(g + 1) * HEAD_DIM], cos, sin) for g in range(N_KV_HEADS)]
    out = jnp.concatenate(parts + [kv[:, KV_WIDTH:]], axis=1)
    inside = (i > 0) & (i < pl.num_programs(0) - 1)
    o_ref[...] = jnp.where(inside, out, 0.0).astype(BF16)


def _kv_projection(h_or_x, w_kv, layer, cos_pad, sin_pad, input_norm=None):
    s, d = h_or_x.shape
    tile = BRANCH_TILE
    n_tiles = s // tile
    tab_spec = pl.BlockSpec((tile, HEAD_DIM), lambda i: (i, 0))
    row_spec = pl.BlockSpec((tile, d), lambda i: (jnp.clip(i - 1, 0, n_tiles - 1), 0))
    kv_spec = pl.BlockSpec((tile, 2 * KV_WIDTH), lambda i: (i, 0))
    kv_shape = jax.ShapeDtypeStruct(((n_tiles + 2) * tile, 2 * KV_WIDTH), BF16)
    common = [_layer_resident(layer, (d, 2 * KV_WIDTH)), tab_spec, tab_spec]
    if input_norm is None:
        return pl.pallas_call(
            _kv_kernel,
            grid=(n_tiles + 2,),
            in_specs=[row_spec] + common,
            out_specs=kv_spec,
            out_shape=kv_shape,
            compiler_params=_params("parallel"),
            name="kv_projection",
        )(h_or_x, w_kv, cos_pad, sin_pad)
    g, b = input_norm
    return pl.pallas_call(
        _norm_kv_kernel,
        grid=(n_tiles + 2,),
        in_specs=[row_spec, _resident((1, d)), _resident((1, d))] + common,
        out_specs=[kv_spec, row_spec, row_spec],
        out_shape=[kv_shape, jax.ShapeDtypeStruct((s, d), F32), jax.ShapeDtypeStruct((s, d), BF16)],
        compiler_params=_params("arbitrary"),
        name="norm_kv_projection",
    )(h_or_x, g.reshape(1, d), b.reshape(1, d), w_kv, cos_pad, sin_pad)


def _attn_branch_kernel(sink_ref, h_ref, w_ref, wc_ref, kv_ref, kvp_ref, kvn_ref, cos_ref, sin_ref, o_ref,
                        p0_ref, p1_ref, y0_ref, y1_ref):
    tile = BRANCH_TILE
    n_blocks = tile // BLOCK
    s = pl.program_id(0)
    n_tiles = 2 * (pl.num_programs(0) - 1)
    scale = HEAD_DIM ** -0.5 * LOG2_E

    @pl.when(s == 0)
    def _():
        p1_ref[...] = jnp.zeros_like(p1_ref)

    q_row = lax.broadcasted_iota(jnp.int32, (GROUP * BLOCK, 3 * BLOCK), 0) % BLOCK
    k_col = lax.broadcasted_iota(jnp.int32, (GROUP * BLOCK, 3 * BLOCK), 1)
    band = (k_col >= q_row) & (k_col <= q_row + 2 * BLOCK)
    head_of_row = lax.broadcasted_iota(jnp.int32, (GROUP * BLOCK, 1), 0) // BLOCK

    def band_rows(w, b, cols):
        first = w * n_blocks + b
        if first == 0:
            return jnp.concatenate([kvp_ref[:, cols], kv_ref[:2 * BLOCK, cols]], axis=0)
        if first == 2 * n_blocks - 1:
            return jnp.concatenate([kv_ref[(first - 1) * BLOCK:, cols], kvn_ref[:, cols]], axis=0)
        return kv_ref[(first - 1) * BLOCK:(first + 2) * BLOCK, cols]

    units = [(g, b) for g in range(N_KV_HEADS) for b in range(n_blocks)]
    n_jobs = len(units) // UNITS_PER_JOB
    job_cols = 2 * ATTN_WIDTH // n_jobs
    out_cols = D_MODEL // n_jobs

    def attend(p_ref, y_ref, w, tile_idx, mxu_jobs):
        rows = slice(w * tile, (w + 1) * tile)
        cos_q = cos_ref[rows, :]
        sin_q = sin_ref[rows, :]
        q_heads = {}

        def scores_of(g, b):
            if g not in q_heads:
                q_heads[g] = []
                for j in range(GROUP):
                    c0 = (g * GROUP + j) * HEAD_DIM
                    q = p_ref[:, c0:c0 + HEAD_DIM].astype(F32)
                    q_heads[g].append((_rope(q, cos_q, sin_q) * scale).astype(BF16))
            kb = band_rows(w, b, slice(g * HEAD_DIM, (g + 1) * HEAD_DIM))
            q4 = jnp.concatenate([qh[b * BLOCK:(b + 1) * BLOCK] for qh in q_heads[g]], axis=0)
            return lax.dot_general(q4, kb, (((1,), (1,)), ((), ())), preferred_element_type=F32)

        scores = scores_of(*units[0])
        for u, (g, b) in enumerate(units):
            mxu_jobs[u]()
            r = slice(b * BLOCK, (b + 1) * BLOCK)
            sink_col = jnp.zeros((GROUP * BLOCK, 1), F32)
            for j in range(GROUP):
                sink_col = jnp.where(head_of_row == j, sink_ref[g * GROUP + j] * LOG2_E, sink_col)
            valid = band
            if b == 0:
                valid = valid & (k_col >= jnp.where(tile_idx == 0, BLOCK, 0))
            if b == n_blocks - 1:
                valid = valid & (k_col < jnp.where(tile_idx == n_tiles - 1, 2 * BLOCK, 3 * BLOCK))
            scores = jnp.where(valid, scores, -jnp.inf)
            m = jnp.maximum(jnp.max(scores, axis=-1, keepdims=True), sink_col)
            p = jnp.exp2(scores - m)
            denom = jnp.sum(p, axis=-1, keepdims=True) + jnp.exp2(sink_col - m)
            if u + 1 < len(units):
                scores = scores_of(*units[u + 1])
            vb = band_rows(w, b, slice(KV_WIDTH + g * HEAD_DIM, KV_WIDTH + (g + 1) * HEAD_DIM))
            out = jnp.dot(p.astype(BF16), vb, preferred_element_type=F32) / denom
            for j in range(GROUP):
                c0 = (g * GROUP + j) * HEAD_DIM
                zj = p_ref[r, ATTN_WIDTH + c0:ATTN_WIDTH + c0 + HEAD_DIM].astype(F32)
                y_ref[r, c0:c0 + HEAD_DIM] = (out[j * BLOCK:(j + 1) * BLOCK] * _silu(zj)).astype(BF16)

    def project_job(h_rows, p_ref, u):
        def job():
            c = slice(u * job_cols, (u + 1) * job_cols)
            p_ref[:, c] = jnp.dot(h_ref[h_rows, :], w_ref[:, c], preferred_element_type=F32).astype(BF16)
        return job

    def branch_job(y_ref, out_rows, u):
        def job():
            c = slice(u * out_cols, (u + 1) * out_cols)
            o_ref[out_rows, c] = jnp.dot(y_ref[...], wc_ref[:, c], preferred_element_type=F32).astype(BF16)
        return job

    def both(*jobs):
        def job():
            for j in jobs:
                j()
        return job

    def spread(jobs):
        return [jobs[u // UNITS_PER_JOB] if u % UNITS_PER_JOB == 0 else both() for u in range(len(units))]

    lo, hi = slice(0, tile), slice(tile, 2 * tile)
    attend(p1_ref, y1_ref, 0, 2 * s - 1, spread([project_job(lo, p0_ref, j) for j in range(n_jobs)]))
    attend(p0_ref, y0_ref, 1, 2 * s,
           spread([both(project_job(hi, p1_ref, j), branch_job(y1_ref, lo, j)) for j in range(n_jobs)]))
    for j in range(n_jobs):
        branch_job(y0_ref, hi, j)()


def _attention_branch(h_bf16, w_proj, w_branch, layer, sink, kv_pad, cos_pad, sin_pad):
    d = h_bf16.shape[1]
    tile = BRANCH_TILE
    halo_per_pair = 2 * tile // BLOCK
    n_halo = kv_pad.shape[0] // BLOCK
    kv_width = 2 * KV_WIDTH
    tab_spec = pl.BlockSpec((2 * tile, HEAD_DIM), lambda i: (i, 0))
    specs = [pl.BlockSpec((2 * tile, kv_width), lambda i: (i, 0)),
             pl.BlockSpec((BLOCK, kv_width), lambda i: (jnp.maximum(i * halo_per_pair - 1, 0), 0)),
             pl.BlockSpec((BLOCK, kv_width), lambda i: (jnp.minimum((i + 1) * halo_per_pair, n_halo - 1), 0)),
             tab_spec, tab_spec]
    s, _ = h_bf16.shape
    n_pairs = s // (2 * tile)
    return pl.pallas_call(
        _attn_branch_kernel,
        grid=(n_pairs + 1,),
        in_specs=[pl.BlockSpec(memory_space=pltpu.SMEM),
                  pl.BlockSpec((2 * tile, d), lambda i: (jnp.minimum(i, n_pairs - 1), 0)),
                  _layer_resident(layer, (d, 2 * ATTN_WIDTH)), _layer_resident(layer, (ATTN_WIDTH, d))] + specs,
        out_specs=pl.BlockSpec((2 * tile, d), lambda i: (i, 0)),
        out_shape=jax.ShapeDtypeStruct(((n_pairs + 1) * 2 * tile, d), BF16),
        scratch_shapes=[pltpu.VMEM((tile, 2 * ATTN_WIDTH), BF16), pltpu.VMEM((tile, 2 * ATTN_WIDTH), BF16),
                        pltpu.VMEM((tile, ATTN_WIDTH), BF16), pltpu.VMEM((tile, ATTN_WIDTH), BF16)],
        compiler_params=_params("arbitrary"),
        name="attention_branch",
    )(sink, h_bf16, w_proj, w_branch, kv_pad, kv_pad, kv_pad, cos_pad, sin_pad)


def _final_kernel(cab0_ref, cab1_ref, cc0_ref, cc1_ref, hb_ref, h_ref, wr_ref, wo_ref, gb_ref, g_ref, b_ref,
                  *out_refs):
    tile = BRANCH_TILE
    for t, (cab_ref, cc_ref) in enumerate(((cab0_ref, cc0_ref), (cab1_ref, cc1_ref))):
        rows = slice(t * tile, (t + 1) * tile)
        gate_c = _sigmoid(jnp.dot(hb_ref[rows, :], wr_ref[...], preferred_element_type=F32) + gb_ref[...])
        merged = cab_ref[...].astype(F32) + gate_c * cc_ref[...].astype(F32)
        out = jnp.dot(merged.astype(BF16), wo_ref[...], preferred_element_type=F32)
        y = _layer_norm(ALPHA * h_ref[rows, :] + out, g_ref[...], b_ref[...])
        out_refs[0][rows, :] = y
        if len(out_refs) > 1:
            out_refs[1][rows, :] = y.astype(BF16)


def _merge_project_norm(contrib_ab, contrib_c, h_bf16, h, w_gate_c, w_o, layer, gate_bias_c, ln_g, ln_b, emit_bf16):
    s, d = h.shape
    tile = BRANCH_TILE
    row = pl.BlockSpec((2 * tile, d), lambda i: (i, 0))
    shifted = [pl.BlockSpec((tile, d), lambda i: (2 * i + 1, 0)), pl.BlockSpec((tile, d), lambda i: (2 * i + 2, 0))]
    out_specs = [row, row] if emit_bf16 else [row]
    out_shape = [jax.ShapeDtypeStruct((s, d), F32)] + ([jax.ShapeDtypeStruct((s, d), BF16)] if emit_bf16 else [])
    return pl.pallas_call(
        _final_kernel,
        grid=(s // (2 * tile),),
        in_specs=shifted * 2 + [row, row, _layer_resident(layer, (d, d)), _layer_resident(layer, (d, d)),
                                _resident((1, d)), _resident((1, d)), _resident((1, d))],
        out_specs=out_specs,
        out_shape=out_shape,
        compiler_params=_params("parallel"),
        name="merge_project_norm",
    )(contrib_ab, contrib_ab, contrib_c, contrib_c, h_bf16, h, w_gate_c, w_o,
      gate_bias_c.reshape(1, d), ln_g.reshape(1, d), ln_b.reshape(1, d))


def kernel(x, positions, ln0_g, ln0_b, w_in, conv_w, gmlp_ln_g, gmlp_ln_b, spatial_w, spatial_b, sink,
           w_branch_a, w_branch_b, w_branch_c, gate_b, w_out, ln_g, ln_b):
    bsz, s, d = x.shape
    assert (bsz, d) == (1, D_MODEL) and s % 1024 == 0 and w_in.shape == (DEPTH, D_MODEL, IN_WIDTH)

    def tiles(off, width):
        return tuple(range(off // WEIGHT_TILE, (off + width) // WEIGHT_TILE))

    w_conv = _gather_cast_columns(w_in, tiles(OFF_CONV, 4 * CONV_WIDTH) + tiles(OFF_GATE_A, d))
    w_gmlp = _gather_cast_columns(w_in, tiles(OFF_GMLP, 3 * GMLP_WIDTH) + tiles(OFF_GATE_B, d))
    w_qz = _gather_cast_columns(w_in, tiles(OFF_Q, ATTN_WIDTH) + tiles(OFF_ATTN_Z, ATTN_WIDTH))
    w_kv = _gather_cast_columns(w_in, tiles(OFF_KV, 2 * KV_WIDTH))
    w_gate_c = _gather_cast_columns(w_in, tiles(OFF_GATE_C, d))
    wa, wb, wc, wo = (_cast_bf16(w) for w in (w_branch_a, w_branch_b, w_branch_c, w_out))

    cos_pad, sin_pad = _rope_tables(jnp.pad(positions.reshape(s), (BRANCH_TILE, BRANCH_TILE)))
    h = h_bf16 = None
    for l in range(DEPTH):
        if l == 0:
            kv_pad, h, h_bf16 = _kv_projection(x.reshape(s, d), w_kv, l, cos_pad, sin_pad, (ln0_g, ln0_b))
        else:
            kv_pad = _kv_projection(h_bf16, w_kv, l, cos_pad, sin_pad)
        contrib_a = _conv_branch(h_bf16, w_conv, wa, l, conv_w[l], gate_b[l, 0])
        contrib_ab = _gmlp_branch(h_bf16, w_gmlp, wb, l, gmlp_ln_g[l], gmlp_ln_b[l],
                                  spatial_w[l], spatial_b[l], gate_b[l, 1], accumulate_onto=contrib_a)
        contrib_c = _attention_branch(h_bf16, w_qz, wc, l, sink[l], kv_pad, cos_pad, sin_pad)
        outs = _merge_project_norm(contrib_ab, contrib_c, h_bf16, h, w_gate_c, wo, l, gate_b[l, 2],
                                   ln_g[l], ln_b[l], emit_bf16=l + 1 < DEPTH)
        h = outs[0]
        h_bf16 = outs[1] if l + 1 < DEPTH else None
    return h.reshape(bsz, s, d)
```

```python
import jax
import jax.numpy as jnp
from jax import lax
from jax.experimental import pallas as pl
from jax.experimental.pallas import tpu as pltpu

D_MODEL = 2048
DEPTH = 2
HEAD_DIM = 128
N_Q_HEADS = 16
N_KV_HEADS = 4
GROUP = N_Q_HEADS // N_KV_HEADS
ATTN_WIDTH = N_Q_HEADS * HEAD_DIM
KV_WIDTH = N_KV_HEADS * HEAD_DIM
CONV_WIDTH = 1024
GMLP_WIDTH = 1024
GMLP_GROUPS = 8
GMLP_GROUP_DIM = GMLP_WIDTH // GMLP_GROUPS
CHUNK = 128
BLOCK = 128
ROPE_THETA = 500000.0
ROPE_DIM = HEAD_DIM // 4
ROPE_HALF = ROPE_DIM // 2
LN_EPS = 1e-5
LOG2_E = 1.4426950408889634
ALPHA = (2.0 * DEPTH) ** 0.25
IN_WIDTH = 4 * CONV_WIDTH + 3 * GMLP_WIDTH + 2 * ATTN_WIDTH + 2 * KV_WIDTH + 3 * D_MODEL

OFF_CONV, OFF_GMLP, OFF_Q, OFF_KV, OFF_ATTN_Z = 0, 4096, 7168, 9216, 10240
OFF_GATE_A, OFF_GATE_B, OFF_GATE_C = 12288, 14336, 16384

WEIGHT_TILE = 1024
PROJ_CHUNK = 1024
BRANCH_TILE = 256
UNITS_PER_JOB = 2
VMEM_LIMIT_BYTES = 56 * 1024 * 1024

F32 = jnp.float32
BF16 = jnp.bfloat16


def _params(*semantics):
    return pltpu.CompilerParams(dimension_semantics=semantics, vmem_limit_bytes=VMEM_LIMIT_BYTES)


def _silu(x):
    return x / (1.0 + jnp.exp(-x))


def _sigmoid(x):
    return 1.0 / (1.0 + jnp.exp(-x))


def _gelu_tanh(x):
    return 0.5 * x * (1.0 + jnp.tanh(0.7978845608028654 * (x + 0.044715 * (x * x * x))))


def _layer_norm(x, g, b):
    mu = jnp.mean(x, axis=-1, keepdims=True)
    xc = x - mu
    var = jnp.mean(xc * xc, axis=-1, keepdims=True)
    return xc * lax.rsqrt(var + LN_EPS) * g + b


def _resident(shape):
    return pl.BlockSpec(shape, lambda i: (0,) * len(shape), pipeline_mode=pl.Buffered(1))


def _layer_resident(layer, shape):
    return pl.BlockSpec((None,) + shape, lambda i: (layer,) + (0,) * len(shape), pipeline_mode=pl.Buffered(1))


def _rope_table_kernel(pos_ref, invf_ref, cos_ref, sin_ref):
    ang = pos_ref[...].astype(F32) * invf_ref[...]
    lane = lax.broadcasted_iota(jnp.int32, ang.shape, 1)
    c = jnp.cos(ang)
    s = jnp.sin(ang)
    cos_ref[...] = jnp.where(lane < ROPE_DIM, c, 1.0)
    sin_ref[...] = jnp.where(lane < ROPE_HALF, -s, jnp.where(lane < ROPE_DIM, s, 0.0))


def _rope_tables(positions, rows=512):
    s = positions.shape[0]
    inv_freq = ROPE_THETA ** (-jnp.arange(ROPE_HALF, dtype=F32) / ROPE_HALF)
    invf = jnp.tile(inv_freq, HEAD_DIM // ROPE_HALF).reshape(1, HEAD_DIM)
    tab = jax.ShapeDtypeStruct((s, HEAD_DIM), F32)
    tab_spec = pl.BlockSpec((rows, HEAD_DIM), lambda i: (i, 0))
    return pl.pallas_call(
        _rope_table_kernel,
        grid=(s // rows,),
        in_specs=[pl.BlockSpec((rows, 1), lambda i: (i, 0)), pl.BlockSpec((1, HEAD_DIM), lambda i: (0, 0))],
        out_specs=[tab_spec, tab_spec],
        out_shape=[tab, tab],
        compiler_params=_params("parallel"),
        name="rope_tables",
    )(positions.reshape(s, 1), invf)


def _rope(t, cos, sin):
    lane = lax.broadcasted_iota(jnp.int32, t.shape, 1)
    partner = jnp.where(lane < ROPE_HALF, pltpu.roll(t, HEAD_DIM - ROPE_HALF, 1), pltpu.roll(t, ROPE_HALF, 1))
    return t * cos + partner * sin


def _cast_kernel(tiles_ref, w_ref, o_ref):
    del tiles_ref
    o_ref[...] = w_ref[...].astype(o_ref.dtype)


def _gather_cast_columns(w, col_tiles, tile=WEIGHT_TILE):
    layers, k, _ = w.shape
    table = jnp.asarray(col_tiles, jnp.int32)
    return pl.pallas_call(
        _cast_kernel,
        grid_spec=pltpu.PrefetchScalarGridSpec(
            num_scalar_prefetch=1,
            grid=(layers, len(col_tiles)),
            in_specs=[pl.BlockSpec((None, k, tile), lambda l, j, t: (l, 0, t[j]))],
            out_specs=pl.BlockSpec((None, k, tile), lambda l, j, t: (l, 0, j)),
        ),
        out_shape=jax.ShapeDtypeStruct((layers, k, len(col_tiles) * tile), BF16),
        compiler_params=_params("parallel", "parallel"),
        name="gather_cast_columns",
    )(table, w)


def _cast_bf16(w):
    return _gather_cast_columns(w, tuple(range(w.shape[-1] // WEIGHT_TILE)))


def _project(h_ref, rows, w_ref, p_ref):
    h = h_ref[rows, :]
    for c in range(0, p_ref.shape[1], PROJ_CHUNK):
        p_ref[:, c:c + PROJ_CHUNK] = jnp.dot(
            h, w_ref[:, c:c + PROJ_CHUNK], preferred_element_type=F32).astype(BF16)


def _branch_call(body, name, h_bf16, w_proj, w_branch, layer, extra_inputs, extra_specs, y_width, out_width):
    s, d = h_bf16.shape
    tile = BRANCH_TILE
    n_pairs = s // (2 * tile)
    proj_width = w_proj.shape[2]
    return pl.pallas_call(
        body,
        grid=(n_pairs,),
        in_specs=[pl.BlockSpec((2 * tile, d), lambda i: (i, 0)),
                  _layer_resident(layer, (d, proj_width)), _layer_resident(layer, w_branch.shape[1:])] + extra_specs,
        out_specs=[pl.BlockSpec((2 * tile, out_width), lambda i: (i, 0)),
                   pl.BlockSpec((tile, out_width), lambda i: (0, 0))],
        out_shape=[jax.ShapeDtypeStruct((s, out_width), BF16), jax.ShapeDtypeStruct((tile, out_width), BF16)],
        scratch_shapes=[pltpu.VMEM((tile, proj_width), BF16), pltpu.VMEM((tile, proj_width), BF16),
                        pltpu.VMEM((tile, y_width), BF16), pltpu.VMEM((tile, y_width), BF16)],
        compiler_params=_params("arbitrary"),
        name=name,
    )(h_bf16, w_proj, w_branch, *extra_inputs)


def _conv_branch_kernel(h_ref, w_ref, wa_ref, cw_ref, gb_ref, o_ref, tail_ref, p0_ref, p1_ref, y0_ref, y1_ref):
    tile = BRANCH_TILE
    s = pl.program_id(0)
    n_tiles = 2 * pl.num_programs(0)
    cw = cw_ref[...]

    @pl.when(s == 0)
    def _():
        p0_ref[...] = jnp.zeros_like(p0_ref)
        p1_ref[...] = jnp.zeros_like(p1_ref)

    def gated_input_row(p_ref, row):
        c = p_ref[row:row + 1, CONV_WIDTH:2 * CONV_WIDTH].astype(F32)
        return c * p_ref[row:row + 1, 2 * CONV_WIDTH:3 * CONV_WIDTH].astype(F32)

    def mix(p_ref, y_ref, out_ref, rows, tile_idx, y_prev, y_next):
        y = p_ref[:, CONV_WIDTH:2 * CONV_WIDTH].astype(F32) * p_ref[:, 2 * CONV_WIDTH:3 * CONV_WIDTH].astype(F32)
        y_prev = jnp.where(tile_idx > 0, y_prev, 0.0)
        y_next = jnp.where(tile_idx < n_tiles - 1, y_next, 0.0)
        row = lax.broadcasted_iota(jnp.int32, y.shape, 0)
        up = jnp.where(row == 0, y_prev, pltpu.roll(y, 1, 0))
        dn = jnp.where(row == tile - 1, y_next, pltpu.roll(y, tile - 1, 0))
        conv = cw[0:1, :] * up + cw[1:2, :] * y + cw[2:3, :] * dn
        z = p_ref[:, 3 * CONV_WIDTH:4 * CONV_WIDTH].astype(F32)
        y_ref[...] = (p_ref[:, :CONV_WIDTH].astype(F32) * conv * _silu(z)).astype(BF16)
        gate = _sigmoid(p_ref[:, 4 * CONV_WIDTH:].astype(F32) + gb_ref[...])
        out_ref[rows, :] = (gate * jnp.dot(y_ref[...], wa_ref[...], preferred_element_type=F32)).astype(BF16)

    lo, hi = slice(0, tile), slice(tile, 2 * tile)
    last_of_tile_before_p1 = gated_input_row(p0_ref, tile - 1)
    _project(h_ref, lo, w_ref, p0_ref)
    last_of_p1 = gated_input_row(p1_ref, tile - 1)
    mix(p1_ref, y1_ref, o_ref, lo, 2 * s - 1, last_of_tile_before_p1, gated_input_row(p0_ref, 0))
    _project(h_ref, hi, w_ref, p1_ref)
    mix(p0_ref, y0_ref, o_ref, hi, 2 * s, last_of_p1, gated_input_row(p1_ref, 0))

    @pl.when(s == pl.num_programs(0) - 1)
    def _():
        mix(p1_ref, y1_ref, tail_ref, slice(None), n_tiles - 1, gated_input_row(p0_ref, tile - 1),
            jnp.zeros((1, CONV_WIDTH), F32))


def _conv_branch(h_bf16, w_proj, w_branch, layer, conv_w, gate_bias):
    d = h_bf16.shape[1]
    return _branch_call(_conv_branch_kernel, "conv_branch", h_bf16, w_proj, w_branch, layer,
                        [conv_w, gate_bias.reshape(1, d)], [_resident((3, CONV_WIDTH)), _resident((1, d))],
                        CONV_WIDTH, d)


def _gmlp_branch_kernel(h_ref, w_ref, wb_ref, g_ref, b_ref, ws_ref, bias_ref, gb_ref, acc_ref, acc_tail_ref,
                        o_ref, tail_ref, p0_ref, p1_ref, y0_ref, y1_ref):
    tile = BRANCH_TILE

    @pl.when(pl.program_id(0) == 0)
    def _():
        p1_ref[...] = jnp.zeros_like(p1_ref)

    proj_width = p0_ref.shape[1]
    n_jobs = 2 + GMLP_GROUPS
    job_cols = proj_width // n_jobs

    def mix(p_ref, y_ref, out_ref, rows, add_ref, h_rows=None, q_ref=None):
        jobs = [slice(u * job_cols, (u + 1) * job_cols) for u in range(n_jobs)] if q_ref is not None else []

        def run_job():
            if q_ref is None:
                return
            c = jobs.pop(0)
            q_ref[:, c] = jnp.dot(h_ref[h_rows, :], w_ref[:, c], preferred_element_type=F32).astype(BF16)

        run_job()
        run_job()
        v = _layer_norm(_gelu_tanh(p_ref[:, GMLP_WIDTH:2 * GMLP_WIDTH].astype(F32)), g_ref[...], b_ref[...])
        v = v.astype(BF16)
        n_chunks = tile // CHUNK
        for g in range(GMLP_GROUPS):
            run_job()
            cols = slice(g * GMLP_GROUP_DIM, (g + 1) * GMLP_GROUP_DIM)
            zcols = slice(2 * GMLP_WIDTH + g * GMLP_GROUP_DIM, 2 * GMLP_WIDTH + (g + 1) * GMLP_GROUP_DIM)
            v_chunks = jnp.concatenate([v[c * CHUNK:(c + 1) * CHUNK, cols] for c in range(n_chunks)], axis=1)
            mixed_chunks = jnp.dot(ws_ref[g], v_chunks, preferred_element_type=F32)
            for c in range(n_chunks):
                r = slice(c * CHUNK, (c + 1) * CHUNK)
                mixed = mixed_chunks[:, c * GMLP_GROUP_DIM:(c + 1) * GMLP_GROUP_DIM] + bias_ref[:, cols]
                u = _gelu_tanh(p_ref[r, cols].astype(F32))
                y_ref[r, cols] = (u * mixed * _silu(p_ref[r, zcols].astype(F32))).astype(BF16)
        assert not jobs
        gate = _sigmoid(p_ref[:, 3 * GMLP_WIDTH:].astype(F32) + gb_ref[...])
        contrib = gate * jnp.dot(y_ref[...], wb_ref[...], preferred_element_type=F32)
        out_ref[rows, :] = (add_ref[rows, :].astype(F32) + contrib).astype(BF16)

    lo, hi = slice(0, tile), slice(tile, 2 * tile)
    mix(p1_ref, y1_ref, o_ref, lo, acc_ref, lo, p0_ref)
    mix(p0_ref, y0_ref, o_ref, hi, acc_ref, hi, p1_ref)

    @pl.when(pl.program_id(0) == pl.num_programs(0) - 1)
    def _():
        mix(p1_ref, y1_ref, tail_ref, slice(None), acc_tail_ref)


def _gmlp_branch(h_bf16, w_proj, w_branch, layer, ln_g, ln_b, spatial_w, spatial_b, gate_bias, accumulate_onto):
    d = h_bf16.shape[1]
    bias = jnp.repeat(spatial_b.T, GMLP_GROUP_DIM, axis=1)
    acc_main, acc_tail = accumulate_onto
    return _branch_call(
        _gmlp_branch_kernel, "gmlp_branch", h_bf16, w_proj, w_branch, layer,
        [ln_g.reshape(1, -1), ln_b.reshape(1, -1), spatial_w.astype(BF16), bias, gate_bias.reshape(1, d),
         acc_main, acc_tail],
        [_resident((1, GMLP_WIDTH)), _resident((1, GMLP_WIDTH)), _resident((GMLP_GROUPS, CHUNK, CHUNK)),
         _resident((CHUNK, GMLP_WIDTH)), _resident((1, d)),
         pl.BlockSpec((2 * BRANCH_TILE, d), lambda i: (i, 0)), _resident((BRANCH_TILE, d))],
        GMLP_WIDTH, d)


def _kv_kernel(h_ref, w_ref, cos_ref, sin_ref, o_ref):
    _kv_body(h_ref[...], w_ref, cos_ref, sin_ref, o_ref)


def _norm_kv_kernel(x_ref, g_ref, b_ref, w_ref, cos_ref, sin_ref, o_ref, hf_ref, hb_ref):
    h = _layer_norm(x_ref[...], g_ref[...], b_ref[...])
    hf_ref[...] = h
    hb = h.astype(BF16)
    hb_ref[...] = hb
    _kv_body(hb, w_ref, cos_ref, sin_ref, o_ref)


def _kv_body(h, w_ref, cos_ref, sin_ref, o_ref):
    i = pl.program_id(0)
    kv = jnp.dot(h, w_ref[...], preferred_element_type=F32)
    cos = cos_ref[...]
    sin = sin_ref[...]
    parts = [_rope(kv[:, g * HEAD_DIM:(g + 1) * HEAD_DIM], cos, sin) for g in range(N_KV_HEADS)]
    out = jnp.concatenate(parts + [kv[:, KV_WIDTH:]], axis=1)
    inside = (i > 0) & (i < pl.num_programs(0) - 1)
    o_ref[...] = jnp.where(inside, out, 0.0).astype(BF16)


def _kv_projection(h_or_x, w_kv, layer, cos_pad, sin_pad, input_norm=None):
    s, d = h_or_x.shape
    tile = BRANCH_TILE
    n_tiles = s // tile
    tab_spec = pl.BlockSpec((tile, HEAD_DIM), lambda i: (i, 0))
    row_spec = pl.BlockSpec((tile, d), lambda i: (jnp.clip(i - 1, 0, n_tiles - 1), 0))
    kv_spec = pl.BlockSpec((tile, 2 * KV_WIDTH), lambda i: (i, 0))
    kv_shape = jax.ShapeDtypeStruct(((n_tiles + 2) * tile, 2 * KV_WIDTH), BF16)
    common = [_layer_resident(layer, (d, 2 * KV_WIDTH)), tab_spec, tab_spec]
    if input_norm is None:
        return pl.pallas_call(
            _kv_kernel,
            grid=(n_tiles + 2,),
            in_specs=[row_spec] + common,
            out_specs=kv_spec,
            out_shape=kv_shape,
            compiler_params=_params("parallel"),
            name="kv_projection",
        )(h_or_x, w_kv, cos_pad, sin_pad)
    g, b = input_norm
    return pl.pallas_call(
        _norm_kv_kernel,
        grid=(n_tiles + 2,),
        in_specs=[row_spec, _resident((1, d)), _resident((1, d))] + common,
        out_specs=[kv_spec, row_spec, row_spec],
        out_shape=[kv_shape, jax.ShapeDtypeStruct((s, d), F32), jax.ShapeDtypeStruct((s, d), BF16)],
        compiler_params=_params("arbitrary"),
        name="norm_kv_projection",
    )(h_or_x, g.reshape(1, d), b.reshape(1, d), w_kv, cos_pad, sin_pad)


def _attn_branch_kernel(sink_ref, h_ref, w_ref, wc_ref, kv_ref, kvp_ref, kvn_ref, cos_ref, sin_ref, o_ref,
                        p0_ref, p1_ref, y0_ref, y1_ref):
    tile = BRANCH_TILE
    n_blocks = tile // BLOCK
    s = pl.program_id(0)
    n_tiles = 2 * (pl.num_programs(0) - 1)
    scale = HEAD_DIM ** -0.5 * LOG2_E

    @pl.when(s == 0)
    def _():
        p1_ref[...] = jnp.zeros_like(p1_ref)

    q_row = lax.broadcasted_iota(jnp.int32, (GROUP * BLOCK, 3 * BLOCK), 0) % BLOCK
    k_col = lax.broadcasted_iota(jnp.int32, (GROUP * BLOCK, 3 * BLOCK), 1)
    band = (k_col >= q_row) & (k_col <= q_row + 2 * BLOCK)
    head_of_row = lax.broadcasted_iota(jnp.int32, (GROUP * BLOCK, 1), 0) // BLOCK

    def band_rows(w, b, cols):
        first = w * n_blocks + b
        if first == 0:
            return jnp.concatenate([kvp_ref[:, cols], kv_ref[:2 * BLOCK, cols]], axis=0)
        if first == 2 * n_blocks - 1:
            return jnp.concatenate([kv_ref[(first - 1) * BLOCK:, cols], kvn_ref[:, cols]], axis=0)
        return kv_ref[(first - 1) * BLOCK:(first + 2) * BLOCK, cols]

    units = [(g, b) for g in range(N_KV_HEADS) for b in range(n_blocks)]
    n_jobs = len(units) // UNITS_PER_JOB
    job_cols = 2 * ATTN_WIDTH // n_jobs
    out_cols = D_MODEL // n_jobs

    def attend(p_ref, y_ref, w, tile_idx, mxu_jobs):
        rows = slice(w * tile, (w + 1) * tile)
        cos_q = cos_ref[rows, :]
        sin_q = sin_ref[rows, :]
        q_heads = {}

        def scores_of(g, b):
            if g not in q_heads:
                q_heads[g] = []
                for j in range(GROUP):
                    c0 = (g * GROUP + j) * HEAD_DIM
                    q = p_ref[:, c0:c0 + HEAD_DIM].astype(F32)
                    q_heads[g].append((_rope(q, cos_q, sin_q) * scale).astype(BF16))
            kb = band_rows(w, b, slice(g * HEAD_DIM, (g + 1) * HEAD_DIM))
            q4 = jnp.concatenate([qh[b * BLOCK:(b + 1) * BLOCK] for qh in q_heads[g]], axis=0)
            return lax.dot_general(q4, kb, (((1,), (1,)), ((), ())), preferred_element_type=F32)

        scores = scores_of(*units[0])
        for u, (g, b) in enumerate(units):
            mxu_jobs[u]()
            r = slice(b * BLOCK, (b + 1) * BLOCK)
            sink_col = jnp.zeros((GROUP * BLOCK, 1), F32)
            for j in range(GROUP):
                sink_col = jnp.where(head_of_row == j, sink_ref[g * GROUP + j] * LOG2_E, sink_col)
            valid = band
            if b == 0:
                valid = valid & (k_col >= jnp.where(tile_idx == 0, BLOCK, 0))
            if b == n_blocks - 1:
                valid = valid & (k_col < jnp.where(tile_idx == n_tiles - 1, 2 * BLOCK, 3 * BLOCK))
            scores = jnp.where(valid, scores, -jnp.inf)
            m = jnp.maximum(jnp.max(scores, axis=-1, keepdims=True), sink_col)
            p = jnp.exp2(scores - m)
            denom = jnp.sum(p, axis=-1, keepdims=True) + jnp.exp2(sink_col - m)
            if u + 1 < len(units):
                scores = scores_of(*units[u + 1])
            vb = band_rows(w, b, slice(KV_WIDTH + g * HEAD_DIM, KV_WIDTH + (g + 1) * HEAD_DIM))
            out = jnp.dot(p.astype(BF16), vb, preferred_element_type=F32) / denom
            for j in range(GROUP):
                c0 = (g * GROUP + j) * HEAD_DIM
                zj = p_ref[r, ATTN_WIDTH + c0:ATTN_WIDTH + c0 + HEAD_DIM].astype(F32)
                y_ref[r, c0:c0 + HEAD_DIM] = (out[j * BLOCK:(j + 1) * BLOCK] * _silu(zj)).astype(BF16)

    def project_job(h_rows, p_ref, u):
        def job():
            c = slice(u * job_cols, (u + 1) * job_cols)
            p_ref[:, c] = jnp.dot(h_ref[h_rows, :], w_ref[:, c], preferred_element_type=F32).astype(BF16)
        return job

    def branch_job(y_ref, out_rows, u):
        def job():
            c = slice(u * out_cols, (u + 1) * out_cols)
            o_ref[out_rows, c] = jnp.dot(y_ref[...], wc_ref[:, c], preferred_element_type=F32).astype(BF16)
        return job

    def both(*jobs):
        def job():
            for j in jobs:
                j()
        return job

    def spread(jobs):
        return [jobs[u // UNITS_PER_JOB] if u % UNITS_PER_JOB == 0 else both() for u in range(len(units))]

    lo, hi = slice(0, tile), slice(tile, 2 * tile)
    attend(p1_ref, y1_ref, 0, 2 * s - 1, spread([project_job(lo, p0_ref, j) for j in range(n_jobs)]))
    attend(p0_ref, y0_ref, 1, 2 * s,
           spread([both(project_job(hi, p1_ref, j), branch_job(y1_ref, lo, j)) for j in range(n_jobs)]))
    for j in range(n_jobs):
        branch_job(y0_ref, hi, j)()


def _attention_branch(h_bf16, w_proj, w_branch, layer, sink, kv_pad, cos_pad, sin_pad):
    d = h_bf16.shape[1]
    tile = BRANCH_TILE
    halo_per_pair = 2 * tile // BLOCK
    kv_width = 2 * KV_WIDTH
    n_halo = kv_pad.shape[0] // BLOCK
    tab_spec = pl.BlockSpec((2 * tile, HEAD_DIM), lambda i: (i, 0))
    specs = [pl.BlockSpec((2 * tile, kv_width), lambda i: (i, 0)),
             pl.BlockSpec((BLOCK, kv_width), lambda i: (jnp.maximum(i * halo_per_pair - 1, 0), 0)),
             pl.BlockSpec((BLOCK, kv_width), lambda i: (jnp.minimum((i + 1) * halo_per_pair, n_halo - 1), 0)),
             tab_spec, tab_spec]
    s, _ = h_bf16.shape
    n_pairs = s // (2 * tile)
    return pl.pallas_call(
        _attn_branch_kernel,
        grid=(n_pairs + 1,),
        in_specs=[pl.BlockSpec(memory_space=pltpu.SMEM),
                  pl.BlockSpec((2 * tile, d), lambda i: (jnp.minimum(i, n_pairs - 1), 0)),
                  _layer_resident(layer, (d, 2 * ATTN_WIDTH)), _layer_resident(layer, (ATTN_WIDTH, d))] + specs,
        out_specs=pl.BlockSpec((2 * tile, d), lambda i: (i, 0)),
        out_shape=jax.ShapeDtypeStruct(((n_pairs + 1) * 2 * tile, d), BF16),
        scratch_shapes=[pltpu.VMEM((tile, 2 * ATTN_WIDTH), BF16), pltpu.VMEM((tile, 2 * ATTN_WIDTH), BF16),
                        pltpu.VMEM((tile, ATTN_WIDTH), BF16), pltpu.VMEM((tile, ATTN_WIDTH), BF16)],
        compiler_params=_params("arbitrary"),
        name="attention_branch",
    )(sink, h_bf16, w_proj, w_branch, kv_pad, kv_pad, kv_pad, cos_pad, sin_pad)


def _final_kernel(cab0_ref, cab1_ref, cab_tail_ref, cc0_ref, cc1_ref, hb_ref, h_ref, wr_ref, wo_ref,
                  gb_ref, g_ref, b_ref, *out_refs):
    tile = BRANCH_TILE
    is_last = pl.program_id(0) == pl.num_programs(0) - 1
    for t, (cab_ref, cc_ref) in enumerate(((cab0_ref, cc0_ref), (cab1_ref, cc1_ref))):
        rows = slice(t * tile, (t + 1) * tile)
        cab, cc = cab_ref[...], cc_ref[...]
        if t == 1:
            cab = jnp.where(is_last, cab_tail_ref[...], cab)
        gate_c = _sigmoid(jnp.dot(hb_ref[rows, :], wr_ref[...], preferred_element_type=F32) + gb_ref[...])
        merged = cab.astype(F32) + gate_c * cc.astype(F32)
        out = jnp.dot(merged.astype(BF16), wo_ref[...], preferred_element_type=F32)
        y = _layer_norm(ALPHA * h_ref[rows, :] + out, g_ref[...], b_ref[...])
        out_refs[0][rows, :] = y
        if len(out_refs) > 1:
            out_refs[1][rows, :] = y.astype(BF16)


def _merge_project_norm(contrib_ab, contrib_c, h_bf16, h, w_gate_c, w_o, layer, gate_bias_c, ln_g, ln_b, emit_bf16):
    s, d = h.shape
    tile = BRANCH_TILE
    n_tiles = s // tile
    row = pl.BlockSpec((2 * tile, d), lambda i: (i, 0))
    first = pl.BlockSpec((tile, d), lambda i: (2 * i + 1, 0))
    second = pl.BlockSpec((tile, d), lambda i: (2 * i + 2, 0))
    second_clamped = pl.BlockSpec((tile, d), lambda i: (jnp.minimum(2 * i + 2, n_tiles - 1), 0))
    out_specs = [row, row] if emit_bf16 else [row]
    out_shape = [jax.ShapeDtypeStruct((s, d), F32)] + ([jax.ShapeDtypeStruct((s, d), BF16)] if emit_bf16 else [])
    return pl.pallas_call(
        _final_kernel,
        grid=(s // (2 * tile),),
        in_specs=[first, second_clamped, _resident((tile, d)), first, second] + [row, row, _layer_resident(layer, (d, d)), _layer_resident(layer, (d, d)),
                                _resident((1, d)), _resident((1, d)), _resident((1, d))],
        out_specs=out_specs,
        out_shape=out_shape,
        compiler_params=_params("parallel"),
        name="merge_project_norm",
    )(contrib_ab[0], contrib_ab[0], contrib_ab[1], contrib_c, contrib_c, h_bf16, h, w_gate_c, w_o,
      gate_bias_c.reshape(1, d), ln_g.reshape(1, d), ln_b.reshape(1, d))


def kernel(x, positions, ln0_g, ln0_b, w_in, conv_w, gmlp_ln_g, gmlp_ln_b, spatial_w, spatial_b, sink,
           w_branch_a, w_branch_b, w_branch_c, gate_b, w_out, ln_g, ln_b):
    bsz, s, d = x.shape
    assert (bsz, d) == (1, D_MODEL) and s % 1024 == 0 and w_in.shape == (DEPTH, D_MODEL, IN_WIDTH)

    def tiles(off, width):
        return tuple(range(off // WEIGHT_TILE, (off + width) // WEIGHT_TILE))

    w_conv = _gather_cast_columns(w_in, tiles(OFF_CONV, 4 * CONV_WIDTH) + tiles(OFF_GATE_A, d))
    w_gmlp = _gather_cast_columns(w_in, tiles(OFF_GMLP, 3 * GMLP_WIDTH) + tiles(OFF_GATE_B, d))
    w_qz = _gather_cast_columns(w_in, tiles(OFF_Q, ATTN_WIDTH) + tiles(OFF_ATTN_Z, ATTN_WIDTH))
    w_kv = _gather_cast_columns(w_in, tiles(OFF_KV, 2 * KV_WIDTH))
    w_gate_c = _gather_cast_columns(w_in, tiles(OFF_GATE_C, d))
    wa, wb, wc, wo = (_cast_bf16(w) for w in (w_branch_a, w_branch_b, w_branch_c, w_out))

    cos_pad, sin_pad = _rope_tables(jnp.pad(positions.reshape(s), (BRANCH_TILE, BRANCH_TILE)))
    h = h_bf16 = None
    for l in range(DEPTH):
        if l == 0:
            kv_pad, h, h_bf16 = _kv_projection(x.reshape(s, d), w_kv, l, cos_pad, sin_pad, (ln0_g, ln0_b))
        else:
            kv_pad = _kv_projection(h_bf16, w_kv, l, cos_pad, sin_pad)
        contrib_a = _conv_branch(h_bf16, w_conv, wa, l, conv_w[l], gate_b[l, 0])
        contrib_ab = _gmlp_branch(h_bf16, w_gmlp, wb, l, gmlp_ln_g[l], gmlp_ln_b[l],
                                  spatial_w[l], spatial_b[l], gate_b[l, 1], accumulate_onto=contrib_a)
        contrib_c = _attention_branch(h_bf16, w_qz, wc, l, sink[l], kv_pad, cos_pad, sin_pad)
        outs = _merge_project_norm(contrib_ab, contrib_c, h_bf16, h, w_gate_c, wo, l, gate_b[l, 2],
                                   ln_g[l], ln_b[l], emit_bf16=l + 1 < DEPTH)
        h = outs[0]
        h_bf16 = outs[1] if l + 1 < DEPTH else None
    return h.reshape(bsz, s, d)
```

```python
import jax
import jax.numpy as jnp
from jax import lax
from jax.experimental import pallas as pl
from jax.experimental.pallas import tpu as pltpu

D_MODEL = 2048
DEPTH = 2
HEAD_DIM = 128
N_Q_HEADS = 16
N_KV_HEADS = 4
GROUP = N_Q_HEADS // N_KV_HEADS
ATTN_WIDTH = N_Q_HEADS * HEAD_DIM
KV_WIDTH = N_KV_HEADS * HEAD_DIM
CONV_WIDTH = 1024
GMLP_WIDTH = 1024
GMLP_GROUPS = 8
GMLP_GROUP_DIM = GMLP_WIDTH // GMLP_GROUPS
CHUNK = 128
BLOCK = 128
ROPE_THETA = 500000.0
ROPE_DIM = HEAD_DIM // 4
ROPE_HALF = ROPE_DIM // 2
LN_EPS = 1e-5
LOG2_E = 1.4426950408889634
ALPHA = (2.0 * DEPTH) ** 0.25
IN_WIDTH = 4 * CONV_WIDTH + 3 * GMLP_WIDTH + 2 * ATTN_WIDTH + 2 * KV_WIDTH + 3 * D_MODEL

OFF_CONV, OFF_GMLP, OFF_Q, OFF_KV, OFF_ATTN_Z = 0, 4096, 7168, 9216, 10240
OFF_GATE_A, OFF_GATE_B, OFF_GATE_C = 12288, 14336, 16384

WEIGHT_TILE = 1024
PROJ_CHUNK = 1024
BRANCH_TILE = 256
UNITS_PER_JOB = 2
VMEM_LIMIT_BYTES = 56 * 1024 * 1024

F32 = jnp.float32
BF16 = jnp.bfloat16


def _params(*semantics):
    return pltpu.CompilerParams(dimension_semantics=semantics, vmem_limit_bytes=VMEM_LIMIT_BYTES)


def _silu(x):
    return x / (1.0 + jnp.exp(-x))


def _sigmoid(x):
    return 1.0 / (1.0 + jnp.exp(-x))


def _gelu_tanh(x):
    return 0.5 * x * (1.0 + jnp.tanh(0.7978845608028654 * (x + 0.044715 * (x * x * x))))


def _layer_norm(x, g, b):
    mu = jnp.mean(x, axis=-1, keepdims=True)
    xc = x - mu
    var = jnp.mean(xc * xc, axis=-1, keepdims=True)
    return xc * lax.rsqrt(var + LN_EPS) * g + b


def _resident(shape):
    return pl.BlockSpec(shape, lambda i: (0,) * len(shape), pipeline_mode=pl.Buffered(1))


def _layer_resident(layer, shape):
    return pl.BlockSpec((None,) + shape, lambda i: (layer,) + (0,) * len(shape), pipeline_mode=pl.Buffered(1))


def _rope_table(pos, invf):
    ang = pos.astype(F32) * invf
    lane = lax.broadcasted_iota(jnp.int32, ang.shape, 1)
    c = jnp.cos(ang)
    s = jnp.sin(ang)
    return jnp.where(lane < ROPE_DIM, c, 1.0), jnp.where(lane < ROPE_HALF, -s, jnp.where(lane < ROPE_DIM, s, 0.0))


def _lane_frequencies():
    inv_freq = ROPE_THETA ** (-jnp.arange(ROPE_HALF, dtype=F32) / ROPE_HALF)
    return jnp.tile(inv_freq, HEAD_DIM // ROPE_HALF).reshape(1, HEAD_DIM)


def _rope(t, cos, sin):
    lane = lax.broadcasted_iota(jnp.int32, t.shape, 1)
    partner = jnp.where(lane < ROPE_HALF, pltpu.roll(t, HEAD_DIM - ROPE_HALF, 1), pltpu.roll(t, ROPE_HALF, 1))
    return t * cos + partner * sin


def _cast_kernel(tiles_ref, w_ref, o_ref):
    del tiles_ref
    o_ref[...] = w_ref[...].astype(o_ref.dtype)


def _gather_cast_columns(w, col_tiles, tile=WEIGHT_TILE):
    layers, k, _ = w.shape
    table = jnp.asarray(col_tiles, jnp.int32)
    return pl.pallas_call(
        _cast_kernel,
        grid_spec=pltpu.PrefetchScalarGridSpec(
            num_scalar_prefetch=1,
            grid=(layers, len(col_tiles)),
            in_specs=[pl.BlockSpec((None, k, tile), lambda l, j, t: (l, 0, t[j]))],
            out_specs=pl.BlockSpec((None, k, tile), lambda l, j, t: (l, 0, j)),
        ),
        out_shape=jax.ShapeDtypeStruct((layers, k, len(col_tiles) * tile), BF16),
        compiler_params=_params("parallel", "parallel"),
        name="gather_cast_columns",
    )(table, w)


def _cast_bf16(w):
    return _gather_cast_columns(w, tuple(range(w.shape[-1] // WEIGHT_TILE)))


def _project(h_ref, rows, w_ref, p_ref):
    h = h_ref[rows, :]
    for c in range(0, p_ref.shape[1], PROJ_CHUNK):
        p_ref[:, c:c + PROJ_CHUNK] = jnp.dot(
            h, w_ref[:, c:c + PROJ_CHUNK], preferred_element_type=F32).astype(BF16)


def _branch_call(body, name, h_bf16, w_proj, w_branch, layer, extra_inputs, extra_specs, y_width, out_width):
    s, d = h_bf16.shape
    tile = BRANCH_TILE
    n_pairs = s // (2 * tile)
    proj_width = w_proj.shape[2]
    return pl.pallas_call(
        body,
        grid=(n_pairs,),
        in_specs=[pl.BlockSpec((2 * tile, d), lambda i: (i, 0)),
                  _layer_resident(layer, (d, proj_width)), _layer_resident(layer, w_branch.shape[1:])] + extra_specs,
        out_specs=[pl.BlockSpec((2 * tile, out_width), lambda i: (i, 0)),
                   pl.BlockSpec((tile, out_width), lambda i: (0, 0))],
        out_shape=[jax.ShapeDtypeStruct((s, out_width), BF16), jax.ShapeDtypeStruct((tile, out_width), BF16)],
        scratch_shapes=[pltpu.VMEM((tile, proj_width), BF16), pltpu.VMEM((tile, proj_width), BF16),
                        pltpu.VMEM((tile, y_width), BF16), pltpu.VMEM((tile, y_width), BF16)],
        compiler_params=_params("arbitrary"),
        name=name,
    )(h_bf16, w_proj, w_branch, *extra_inputs)


def _conv_branch_kernel(h_ref, w_ref, wa_ref, cw_ref, gb_ref, o_ref, tail_ref, p0_ref, p1_ref, y0_ref, y1_ref):
    tile = BRANCH_TILE
    s = pl.program_id(0)
    n_tiles = 2 * pl.num_programs(0)
    cw = cw_ref[...]

    @pl.when(s == 0)
    def _():
        p0_ref[...] = jnp.zeros_like(p0_ref)
        p1_ref[...] = jnp.zeros_like(p1_ref)

    def gated_input_row(p_ref, row):
        c = p_ref[row:row + 1, CONV_WIDTH:2 * CONV_WIDTH].astype(F32)
        return c * p_ref[row:row + 1, 2 * CONV_WIDTH:3 * CONV_WIDTH].astype(F32)

    def mix(p_ref, y_ref, out_ref, rows, tile_idx, y_prev, y_next):
        y = p_ref[:, CONV_WIDTH:2 * CONV_WIDTH].astype(F32) * p_ref[:, 2 * CONV_WIDTH:3 * CONV_WIDTH].astype(F32)
        y_prev = jnp.where(tile_idx > 0, y_prev, 0.0)
        y_next = jnp.where(tile_idx < n_tiles - 1, y_next, 0.0)
        row = lax.broadcasted_iota(jnp.int32, y.shape, 0)
        up = jnp.where(row == 0, y_prev, pltpu.roll(y, 1, 0))
        dn = jnp.where(row == tile - 1, y_next, pltpu.roll(y, tile - 1, 0))
        conv = cw[0:1, :] * up + cw[1:2, :] * y + cw[2:3, :] * dn
        z = p_ref[:, 3 * CONV_WIDTH:4 * CONV_WIDTH].astype(F32)
        y_ref[...] = (p_ref[:, :CONV_WIDTH].astype(F32) * conv * _silu(z)).astype(BF16)
        gate = _sigmoid(p_ref[:, 4 * CONV_WIDTH:].astype(F32) + gb_ref[...])
        out_ref[rows, :] = (gate * jnp.dot(y_ref[...], wa_ref[...], preferred_element_type=F32)).astype(BF16)

    lo, hi = slice(0, tile), slice(tile, 2 * tile)
    last_of_tile_before_p1 = gated_input_row(p0_ref, tile - 1)
    _project(h_ref, lo, w_ref, p0_ref)
    last_of_p1 = gated_input_row(p1_ref, tile - 1)
    mix(p1_ref, y1_ref, o_ref, lo, 2 * s - 1, last_of_tile_before_p1, gated_input_row(p0_ref, 0))
    _project(h_ref, hi, w_ref, p1_ref)
    mix(p0_ref, y0_ref, o_ref, hi, 2 * s, last_of_p1, gated_input_row(p1_ref, 0))

    @pl.when(s == pl.num_programs(0) - 1)
    def _():
        mix(p1_ref, y1_ref, tail_ref, slice(None), n_tiles - 1, gated_input_row(p0_ref, tile - 1),
            jnp.zeros((1, CONV_WIDTH), F32))


def _conv_branch(h_bf16, w_proj, w_branch, layer, conv_w, gate_bias):
    d = h_bf16.shape[1]
    return _branch_call(_conv_branch_kernel, "conv_branch", h_bf16, w_proj, w_branch, layer,
                        [conv_w, gate_bias.reshape(1, d)], [_resident((3, CONV_WIDTH)), _resident((1, d))],
                        CONV_WIDTH, d)


def _gmlp_branch_kernel(h_ref, w_ref, wb_ref, g_ref, b_ref, ws_ref, bias_ref, gb_ref, acc_ref, acc_tail_ref,
                        o_ref, tail_ref, p0_ref, p1_ref, y0_ref, y1_ref):
    tile = BRANCH_TILE

    @pl.when(pl.program_id(0) == 0)
    def _():
        p1_ref[...] = jnp.zeros_like(p1_ref)

    proj_width = p0_ref.shape[1]
    n_jobs = 2 + GMLP_GROUPS
    job_cols = proj_width // n_jobs

    def mix(p_ref, y_ref, out_ref, rows, add_ref, h_rows=None, q_ref=None):
        jobs = [slice(u * job_cols, (u + 1) * job_cols) for u in range(n_jobs)] if q_ref is not None else []

        def run_job():
            if q_ref is None:
                return
            c = jobs.pop(0)
            q_ref[:, c] = jnp.dot(h_ref[h_rows, :], w_ref[:, c], preferred_element_type=F32).astype(BF16)

        run_job()
        run_job()
        v = _layer_norm(_gelu_tanh(p_ref[:, GMLP_WIDTH:2 * GMLP_WIDTH].astype(F32)), g_ref[...], b_ref[...])
        v = v.astype(BF16)
        n_chunks = tile // CHUNK
        for g in range(GMLP_GROUPS):
            run_job()
            cols = slice(g * GMLP_GROUP_DIM, (g + 1) * GMLP_GROUP_DIM)
            zcols = slice(2 * GMLP_WIDTH + g * GMLP_GROUP_DIM, 2 * GMLP_WIDTH + (g + 1) * GMLP_GROUP_DIM)
            v_chunks = jnp.concatenate([v[c * CHUNK:(c + 1) * CHUNK, cols] for c in range(n_chunks)], axis=1)
            mixed_chunks = jnp.dot(ws_ref[g], v_chunks, preferred_element_type=F32)
            for c in range(n_chunks):
                r = slice(c * CHUNK, (c + 1) * CHUNK)
                mixed = mixed_chunks[:, c * GMLP_GROUP_DIM:(c + 1) * GMLP_GROUP_DIM] + bias_ref[:, cols]
                u = _gelu_tanh(p_ref[r, cols].astype(F32))
                y_ref[r, cols] = (u * mixed * _silu(p_ref[r, zcols].astype(F32))).astype(BF16)
        assert not jobs
        gate = _sigmoid(p_ref[:, 3 * GMLP_WIDTH:].astype(F32) + gb_ref[...])
        contrib = gate * jnp.dot(y_ref[...], wb_ref[...], preferred_element_type=F32)
        out_ref[rows, :] = (add_ref[rows, :].astype(F32) + contrib).astype(BF16)

    lo, hi = slice(0, tile), slice(tile, 2 * tile)
    mix(p1_ref, y1_ref, o_ref, lo, acc_ref, lo, p0_ref)
    mix(p0_ref, y0_ref, o_ref, hi, acc_ref, hi, p1_ref)

    @pl.when(pl.program_id(0) == pl.num_programs(0) - 1)
    def _():
        mix(p1_ref, y1_ref, tail_ref, slice(None), acc_tail_ref)


def _gmlp_branch(h_bf16, w_proj, w_branch, layer, ln_g, ln_b, spatial_w, spatial_b, gate_bias, accumulate_onto):
    d = h_bf16.shape[1]
    bias = jnp.repeat(spatial_b.T, GMLP_GROUP_DIM, axis=1)
    acc_main, acc_tail = accumulate_onto
    return _branch_call(
        _gmlp_branch_kernel, "gmlp_branch", h_bf16, w_proj, w_branch, layer,
        [ln_g.reshape(1, -1), ln_b.reshape(1, -1), spatial_w.astype(BF16), bias, gate_bias.reshape(1, d),
         acc_main, acc_tail],
        [_resident((1, GMLP_WIDTH)), _resident((1, GMLP_WIDTH)), _resident((GMLP_GROUPS, CHUNK, CHUNK)),
         _resident((CHUNK, GMLP_WIDTH)), _resident((1, d)),
         pl.BlockSpec((2 * BRANCH_TILE, d), lambda i: (i, 0)), _resident((BRANCH_TILE, d))],
        GMLP_WIDTH, d)


def _kv_kernel(h_ref, w_ref, cos_ref, sin_ref, o_ref):
    _kv_body(h_ref[...], w_ref, cos_ref[...], sin_ref[...], o_ref)


def _norm_kv_kernel(x_ref, g_ref, b_ref, w_ref, pos_ref, invf_ref, o_ref, hf_ref, hb_ref, cos_ref, sin_ref):
    h = _layer_norm(x_ref[...], g_ref[...], b_ref[...])
    hf_ref[...] = h
    hb = h.astype(BF16)
    hb_ref[...] = hb
    cos, sin = _rope_table(pos_ref[...], invf_ref[...])
    cos_ref[...] = cos
    sin_ref[...] = sin
    _kv_body(hb, w_ref, cos, sin, o_ref)


def _kv_body(h, w_ref, cos, sin, o_ref):
    i = pl.program_id(0)
    kv = jnp.dot(h, w_ref[...], preferred_element_type=F32)
    parts = [_rope(kv[:, g * HEAD_DIM:(g + 1) * HEAD_DIM], cos, sin) for g in range(N_KV_HEADS)]
    out = jnp.concatenate(parts + [kv[:, KV_WIDTH:]], axis=1)
    inside = (i > 0) & (i < pl.num_programs(0) - 1)
    o_ref[...] = jnp.where(inside, out, 0.0).astype(BF16)


def _kv_projection(h_bf16, w_kv, layer, cos_pad, sin_pad):
    s, d = h_bf16.shape
    tile = BRANCH_TILE
    n_tiles = s // tile
    tab_spec = pl.BlockSpec((tile, HEAD_DIM), lambda i: (i, 0))
    return pl.pallas_call(
        _kv_kernel,
        grid=(n_tiles + 2,),
        in_specs=[pl.BlockSpec((tile, d), lambda i: (jnp.clip(i - 1, 0, n_tiles - 1), 0)),
                  _layer_resident(layer, (d, 2 * KV_WIDTH)), tab_spec, tab_spec],
        out_specs=pl.BlockSpec((tile, 2 * KV_WIDTH), lambda i: (i, 0)),
        out_shape=jax.ShapeDtypeStruct(((n_tiles + 2) * tile, 2 * KV_WIDTH), BF16),
        compiler_params=_params("parallel"),
        name="kv_projection",
    )(h_bf16, w_kv, cos_pad, sin_pad)


def _norm_kv_projection(x, ln_g, ln_b, positions, w_kv, layer):
    s, d = x.shape
    tile = BRANCH_TILE
    n_tiles = s // tile
    padded = (n_tiles + 2) * tile
    positions_pad = jnp.pad(positions, (tile, tile)).reshape(padded, 1)
    tab_spec = pl.BlockSpec((tile, HEAD_DIM), lambda i: (i, 0))
    tab_shape = jax.ShapeDtypeStruct((padded, HEAD_DIM), F32)
    row_spec = pl.BlockSpec((tile, d), lambda i: (jnp.clip(i - 1, 0, n_tiles - 1), 0))
    return pl.pallas_call(
        _norm_kv_kernel,
        grid=(n_tiles + 2,),
        in_specs=[row_spec, _resident((1, d)), _resident((1, d)), _layer_resident(layer, (d, 2 * KV_WIDTH)),
                  pl.BlockSpec((tile, 1), lambda i: (i, 0)), _resident((1, HEAD_DIM))],
        out_specs=[pl.BlockSpec((tile, 2 * KV_WIDTH), lambda i: (i, 0)), row_spec, row_spec, tab_spec, tab_spec],
        out_shape=[jax.ShapeDtypeStruct((padded, 2 * KV_WIDTH), BF16), jax.ShapeDtypeStruct((s, d), F32),
                   jax.ShapeDtypeStruct((s, d), BF16), tab_shape, tab_shape],
        compiler_params=_params("arbitrary"),
        name="norm_kv_projection",
    )(x, ln_g.reshape(1, d), ln_b.reshape(1, d), w_kv, positions_pad, _lane_frequencies())


def _attn_branch_kernel(sink_ref, h_ref, w_ref, wc_ref, kv_ref, kvp_ref, kvn_ref, cos_ref, sin_ref, o_ref,
                        p0_ref, p1_ref, y0_ref, y1_ref):
    tile = BRANCH_TILE
    n_blocks = tile // BLOCK
    s = pl.program_id(0)
    n_tiles = 2 * (pl.num_programs(0) - 1)
    scale = HEAD_DIM ** -0.5 * LOG2_E

    @pl.when(s == 0)
    def _():
        p1_ref[...] = jnp.zeros_like(p1_ref)

    q_row = lax.broadcasted_iota(jnp.int32, (GROUP * BLOCK, 3 * BLOCK), 0) % BLOCK
    k_col = lax.broadcasted_iota(jnp.int32, (GROUP * BLOCK, 3 * BLOCK), 1)
    band = (k_col >= q_row) & (k_col <= q_row + 2 * BLOCK)
    head_of_row = lax.broadcasted_iota(jnp.int32, (GROUP * BLOCK, 1), 0) // BLOCK

    def band_rows(w, b, cols):
        first = w * n_blocks + b
        if first == 0:
            return jnp.concatenate([kvp_ref[:, cols], kv_ref[:2 * BLOCK, cols]], axis=0)
        if first == 2 * n_blocks - 1:
            return jnp.concatenate([kv_ref[(first - 1) * BLOCK:, cols], kvn_ref[:, cols]], axis=0)
        return kv_ref[(first - 1) * BLOCK:(first + 2) * BLOCK, cols]

    units = [(g, b) for g in range(N_KV_HEADS) for b in range(n_blocks)]
    n_jobs = len(units) // UNITS_PER_JOB
    job_cols = 2 * ATTN_WIDTH // n_jobs
    out_cols = D_MODEL // n_jobs

    def attend(p_ref, y_ref, w, tile_idx, mxu_jobs):
        rows = slice(w * tile, (w + 1) * tile)
        cos_q = cos_ref[rows, :]
        sin_q = sin_ref[rows, :]
        q_heads = {}

        def scores_of(g, b):
            if g not in q_heads:
                q_heads[g] = []
                for j in range(GROUP):
                    c0 = (g * GROUP + j) * HEAD_DIM
                    q = p_ref[:, c0:c0 + HEAD_DIM].astype(F32)
                    q_heads[g].append((_rope(q, cos_q, sin_q) * scale).astype(BF16))
            kb = band_rows(w, b, slice(g * HEAD_DIM, (g + 1) * HEAD_DIM))
            q4 = jnp.concatenate([qh[b * BLOCK:(b + 1) * BLOCK] for qh in q_heads[g]], axis=0)
            return lax.dot_general(q4, kb, (((1,), (1,)), ((), ())), preferred_element_type=F32)

        scores = scores_of(*units[0])
        for u, (g, b) in enumerate(units):
            mxu_jobs[u]()
            r = slice(b * BLOCK, (b + 1) * BLOCK)
            sink_col = jnp.zeros((GROUP * BLOCK, 1), F32)
            for j in range(GROUP):
                sink_col = jnp.where(head_of_row == j, sink_ref[g * GROUP + j] * LOG2_E, sink_col)
            valid = band
            if b == 0:
                valid = valid & (k_col >= jnp.where(tile_idx == 0, BLOCK, 0))
            if b == n_blocks - 1:
                valid = valid & (k_col < jnp.where(tile_idx == n_tiles - 1, 2 * BLOCK, 3 * BLOCK))
            scores = jnp.where(valid, scores, -jnp.inf)
            m = jnp.maximum(jnp.max(scores, axis=-1, keepdims=True), sink_col)
            p = jnp.exp2(scores - m)
            denom = jnp.sum(p, axis=-1, keepdims=True) + jnp.exp2(sink_col - m)
            if u + 1 < len(units):
                scores = scores_of(*units[u + 1])
            vb = band_rows(w, b, slice(KV_WIDTH + g * HEAD_DIM, KV_WIDTH + (g + 1) * HEAD_DIM))
            out = jnp.dot(p.astype(BF16), vb, preferred_element_type=F32) / denom
            for j in range(GROUP):
                c0 = (g * GROUP + j) * HEAD_DIM
                zj = p_ref[r, ATTN_WIDTH + c0:ATTN_WIDTH + c0 + HEAD_DIM].astype(F32)
                y_ref[r, c0:c0 + HEAD_DIM] = (out[j * BLOCK:(j + 1) * BLOCK] * _silu(zj)).astype(BF16)

    def project_job(h_rows, p_ref, u):
        def job():
            c = slice(u * job_cols, (u + 1) * job_cols)
            p_ref[:, c] = jnp.dot(h_ref[h_rows, :], w_ref[:, c], preferred_element_type=F32).astype(BF16)
        return job

    def branch_job(y_ref, out_rows, u):
        def job():
            c = slice(u * out_cols, (u + 1) * out_cols)
            o_ref[out_rows, c] = jnp.dot(y_ref[...], wc_ref[:, c], preferred_element_type=F32).astype(BF16)
        return job

    def both(*jobs):
        def job():
            for j in jobs:
                j()
        return job

    def spread(jobs):
        return [jobs[u // UNITS_PER_JOB] if u % UNITS_PER_JOB == 0 else both() for u in range(len(units))]

    lo, hi = slice(0, tile), slice(tile, 2 * tile)
    attend(p1_ref, y1_ref, 0, 2 * s - 1, spread([project_job(lo, p0_ref, j) for j in range(n_jobs)]))
    attend(p0_ref, y0_ref, 1, 2 * s,
           spread([both(project_job(hi, p1_ref, j), branch_job(y1_ref, lo, j)) for j in range(n_jobs)]))
    for j in range(n_jobs):
        branch_job(y0_ref, hi, j)()


def _attention_branch(h_bf16, w_proj, w_branch, layer, sink, kv_pad, cos_pad, sin_pad):
    d = h_bf16.shape[1]
    tile = BRANCH_TILE
    halo_per_pair = 2 * tile // BLOCK
    kv_width = 2 * KV_WIDTH
    n_halo = kv_pad.shape[0] // BLOCK
    tab_spec = pl.BlockSpec((2 * tile, HEAD_DIM), lambda i: (i, 0))
    specs = [pl.BlockSpec((2 * tile, kv_width), lambda i: (i, 0)),
             pl.BlockSpec((BLOCK, kv_width), lambda i: (jnp.maximum(i * halo_per_pair - 1, 0), 0)),
             pl.BlockSpec((BLOCK, kv_width), lambda i: (jnp.minimum((i + 1) * halo_per_pair, n_halo - 1), 0)),
             tab_spec, tab_spec]
    s, _ = h_bf16.shape
    n_pairs = s // (2 * tile)
    return pl.pallas_call(
        _attn_branch_kernel,
        grid=(n_pairs + 1,),
        in_specs=[pl.BlockSpec(memory_space=pltpu.SMEM),
                  pl.BlockSpec((2 * tile, d), lambda i: (jnp.minimum(i, n_pairs - 1), 0)),
                  _layer_resident(layer, (d, 2 * ATTN_WIDTH)), _layer_resident(layer, (ATTN_WIDTH, d))] + specs,
        out_specs=pl.BlockSpec((2 * tile, d), lambda i: (i, 0)),
        out_shape=jax.ShapeDtypeStruct(((n_pairs + 1) * 2 * tile, d), BF16),
        scratch_shapes=[pltpu.VMEM((tile, 2 * ATTN_WIDTH), BF16), pltpu.VMEM((tile, 2 * ATTN_WIDTH), BF16),
                        pltpu.VMEM((tile, ATTN_WIDTH), BF16), pltpu.VMEM((tile, ATTN_WIDTH), BF16)],
        compiler_params=_params("arbitrary"),
        name="attention_branch",
    )(sink, h_bf16, w_proj, w_branch, kv_pad, kv_pad, kv_pad, cos_pad, sin_pad)


def _final_kernel(cab0_ref, cab1_ref, cab_tail_ref, cc0_ref, cc1_ref, hb_ref, h_ref, wr_ref, wo_ref,
                  gb_ref, g_ref, b_ref, *out_refs):
    tile = BRANCH_TILE
    is_last = pl.program_id(0) == pl.num_programs(0) - 1
    for t, (cab_ref, cc_ref) in enumerate(((cab0_ref, cc0_ref), (cab1_ref, cc1_ref))):
        rows = slice(t * tile, (t + 1) * tile)
        cab, cc = cab_ref[...], cc_ref[...]
        if t == 1:
            cab = jnp.where(is_last, cab_tail_ref[...], cab)
        gate_c = _sigmoid(jnp.dot(hb_ref[rows, :], wr_ref[...], preferred_element_type=F32) + gb_ref[...])
        merged = cab.astype(F32) + gate_c * cc.astype(F32)
        out = jnp.dot(merged.astype(BF16), wo_ref[...], preferred_element_type=F32)
        y = _layer_norm(ALPHA * h_ref[rows, :] + out, g_ref[...], b_ref[...])
        out_refs[0][rows, :] = y
        if len(out_refs) > 1:
            out_refs[1][rows, :] = y.astype(BF16)


def _merge_project_norm(contrib_ab, contrib_c, h_bf16, h, w_gate_c, w_o, layer, gate_bias_c, ln_g, ln_b, emit_bf16):
    s, d = h.shape
    tile = BRANCH_TILE
    n_tiles = s // tile
    row = pl.BlockSpec((2 * tile, d), lambda i: (i, 0))
    first = pl.BlockSpec((tile, d), lambda i: (2 * i + 1, 0))
    second = pl.BlockSpec((tile, d), lambda i: (2 * i + 2, 0))
    second_clamped = pl.BlockSpec((tile, d), lambda i: (jnp.minimum(2 * i + 2, n_tiles - 1), 0))
    out_specs = [row, row] if emit_bf16 else [row]
    out_shape = [jax.ShapeDtypeStruct((s, d), F32)] + ([jax.ShapeDtypeStruct((s, d), BF16)] if emit_bf16 else [])
    return pl.pallas_call(
        _final_kernel,
        grid=(s // (2 * tile),),
        in_specs=[first, second_clamped, _resident((tile, d)), first, second, row, row,
                  _layer_resident(layer, (d, d)), _layer_resident(layer, (d, d)),
                  _resident((1, d)), _resident((1, d)), _resident((1, d))],
        out_specs=out_specs,
        out_shape=out_shape,
        compiler_params=_params("parallel"),
        name="merge_project_norm",
    )(contrib_ab[0], contrib_ab[0], contrib_ab[1], contrib_c, contrib_c, h_bf16, h, w_gate_c, w_o,
      gate_bias_c.reshape(1, d), ln_g.reshape(1, d), ln_b.reshape(1, d))


def kernel(x, positions, ln0_g, ln0_b, w_in, conv_w, gmlp_ln_g, gmlp_ln_b, spatial_w, spatial_b, sink,
           w_branch_a, w_branch_b, w_branch_c, gate_b, w_out, ln_g, ln_b):
    bsz, s, d = x.shape
    assert (bsz, d) == (1, D_MODEL) and s % 1024 == 0 and w_in.shape == (DEPTH, D_MODEL, IN_WIDTH)

    def tiles(off, width):
        return tuple(range(off // WEIGHT_TILE, (off + width) // WEIGHT_TILE))

    w_conv = _gather_cast_columns(w_in, tiles(OFF_CONV, 4 * CONV_WIDTH) + tiles(OFF_GATE_A, d))
    w_gmlp = _gather_cast_columns(w_in, tiles(OFF_GMLP, 3 * GMLP_WIDTH) + tiles(OFF_GATE_B, d))
    w_qz = _gather_cast_columns(w_in, tiles(OFF_Q, ATTN_WIDTH) + tiles(OFF_ATTN_Z, ATTN_WIDTH))
    w_kv = _gather_cast_columns(w_in, tiles(OFF_KV, 2 * KV_WIDTH))
    w_gate_c = _gather_cast_columns(w_in, tiles(OFF_GATE_C, d))
    wa, wb, wc, wo = (_cast_bf16(w) for w in (w_branch_a, w_branch_b, w_branch_c, w_out))

    h = h_bf16 = cos_pad = sin_pad = None
    for l in range(DEPTH):
        if l == 0:
            kv_pad, h, h_bf16, cos_pad, sin_pad = _norm_kv_projection(
                x.reshape(s, d), ln0_g, ln0_b, positions.reshape(s), w_kv, l)
        else:
            kv_pad = _kv_projection(h_bf16, w_kv, l, cos_pad, sin_pad)
        contrib_a = _conv_branch(h_bf16, w_conv, wa, l, conv_w[l], gate_b[l, 0])
        contrib_ab = _gmlp_branch(h_bf16, w_gmlp, wb, l, gmlp_ln_g[l], gmlp_ln_b[l],
                                  spatial_w[l], spatial_b[l], gate_b[l, 1], accumulate_onto=contrib_a)
        contrib_c = _attention_branch(h_bf16, w_qz, wc, l, sink[l], kv_pad, cos_pad, sin_pad)
        outs = _merge_project_norm(contrib_ab, contrib_c, h_bf16, h, w_gate_c, wo, l, gate_b[l, 2],
                                   ln_g[l], ln_b[l], emit_bf16=l + 1 < DEPTH)
        h = outs[0]
        h_bf16 = outs[1] if l + 1 < DEPTH else None
    return h.reshape(bsz, s, d)
```

```python
import functools

import jax
import jax.numpy as jnp
from jax import lax
from jax.experimental import pallas as pl
from jax.experimental.pallas import tpu as pltpu

D_MODEL = 2048
DEPTH = 2
HEAD_DIM = 128
N_Q_HEADS = 16
N_KV_HEADS = 4
GROUP = N_Q_HEADS // N_KV_HEADS
ATTN_WIDTH = N_Q_HEADS * HEAD_DIM
KV_WIDTH = N_KV_HEADS * HEAD_DIM
CONV_WIDTH = 1024
GMLP_WIDTH = 1024
GMLP_GROUPS = 8
GMLP_GROUP_DIM = GMLP_WIDTH // GMLP_GROUPS
CHUNK = 128
BLOCK = 128
ROPE_THETA = 500000.0
ROPE_DIM = HEAD_DIM // 4
ROPE_HALF = ROPE_DIM // 2
LN_EPS = 1e-5
LOG2_E = 1.4426950408889634
ALPHA = (2.0 * DEPTH) ** 0.25
IN_WIDTH = 4 * CONV_WIDTH + 3 * GMLP_WIDTH + 2 * ATTN_WIDTH + 2 * KV_WIDTH + 3 * D_MODEL

OFF_CONV, OFF_GMLP, OFF_Q, OFF_KV, OFF_ATTN_Z = 0, 4096, 7168, 9216, 10240
OFF_GATE_A, OFF_GATE_B, OFF_GATE_C = 12288, 14336, 16384

WEIGHT_TILE = 1024
PROJ_CHUNK = 1024
BRANCH_TILE = 256
UNITS_PER_JOB = 2
VMEM_LIMIT_BYTES = 56 * 1024 * 1024

F32 = jnp.float32
BF16 = jnp.bfloat16


def _params(*semantics):
    return pltpu.CompilerParams(dimension_semantics=semantics, vmem_limit_bytes=VMEM_LIMIT_BYTES)


def _silu(x):
    return x / (1.0 + jnp.exp(-x))


def _sigmoid(x):
    return 1.0 / (1.0 + jnp.exp(-x))


def _gelu_tanh(x):
    return 0.5 * x * (1.0 + jnp.tanh(0.7978845608028654 * (x + 0.044715 * (x * x * x))))


def _layer_norm(x, g, b):
    mu = jnp.mean(x, axis=-1, keepdims=True)
    xc = x - mu
    var = jnp.mean(xc * xc, axis=-1, keepdims=True)
    return xc * lax.rsqrt(var + LN_EPS) * g + b


def _resident(shape):
    return pl.BlockSpec(shape, lambda i: (0,) * len(shape), pipeline_mode=pl.Buffered(1))


def _layer_resident(layer, shape):
    return pl.BlockSpec((None,) + shape, lambda i: (layer,) + (0,) * len(shape), pipeline_mode=pl.Buffered(1))


def _rope_table(pos, invf):
    ang = pos.astype(F32) * invf
    lane = lax.broadcasted_iota(jnp.int32, ang.shape, 1)
    c = jnp.cos(ang)
    s = jnp.sin(ang)
    return jnp.where(lane < ROPE_DIM, c, 1.0), jnp.where(lane < ROPE_HALF, -s, jnp.where(lane < ROPE_DIM, s, 0.0))


def _lane_frequencies():
    inv_freq = ROPE_THETA ** (-jnp.arange(ROPE_HALF, dtype=F32) / ROPE_HALF)
    return jnp.tile(inv_freq, HEAD_DIM // ROPE_HALF).reshape(1, HEAD_DIM)


def _rope(t, cos, sin):
    lane = lax.broadcasted_iota(jnp.int32, t.shape, 1)
    partner = jnp.where(lane < ROPE_HALF, pltpu.roll(t, HEAD_DIM - ROPE_HALF, 1), pltpu.roll(t, ROPE_HALF, 1))
    return t * cos + partner * sin


def _cast_kernel(tiles_ref, w_ref, o_ref):
    del tiles_ref
    o_ref[...] = w_ref[...].astype(o_ref.dtype)


def _gather_cast_columns(w, col_tiles, tile=WEIGHT_TILE):
    layers, k, _ = w.shape
    table = jnp.asarray(col_tiles, jnp.int32)
    return pl.pallas_call(
        _cast_kernel,
        grid_spec=pltpu.PrefetchScalarGridSpec(
            num_scalar_prefetch=1,
            grid=(layers, len(col_tiles)),
            in_specs=[pl.BlockSpec((None, k, tile), lambda l, j, t: (l, 0, t[j]))],
            out_specs=pl.BlockSpec((None, k, tile), lambda l, j, t: (l, 0, j)),
        ),
        out_shape=jax.ShapeDtypeStruct((layers, k, len(col_tiles) * tile), BF16),
        compiler_params=_params("parallel", "parallel"),
        name="gather_cast_columns",
    )(table, w)


def _cast_bf16(w):
    return _gather_cast_columns(w, tuple(range(w.shape[-1] // WEIGHT_TILE)))


def _project(h_ref, rows, w_ref, p_ref):
    h = h_ref[rows, :]
    for c in range(0, p_ref.shape[1], PROJ_CHUNK):
        p_ref[:, c:c + PROJ_CHUNK] = jnp.dot(
            h, w_ref[:, c:c + PROJ_CHUNK], preferred_element_type=F32).astype(BF16)


def _branch_call(body, name, h_bf16, w_proj, w_branch, layer, extra_inputs, extra_specs, y_width, out_width):
    s, d = h_bf16.shape
    tile = BRANCH_TILE
    n_pairs = s // (2 * tile)
    proj_width = w_proj.shape[2]
    return pl.pallas_call(
        body,
        grid=(n_pairs,),
        in_specs=[pl.BlockSpec((2 * tile, d), lambda i: (i, 0)),
                  _layer_resident(layer, (d, proj_width)), _layer_resident(layer, w_branch.shape[1:])] + extra_specs,
        out_specs=[pl.BlockSpec((2 * tile, out_width), lambda i: (i, 0)),
                   pl.BlockSpec((tile, out_width), lambda i: (0, 0))],
        out_shape=[jax.ShapeDtypeStruct((s, out_width), BF16), jax.ShapeDtypeStruct((tile, out_width), BF16)],
        scratch_shapes=[pltpu.VMEM((tile, proj_width), BF16), pltpu.VMEM((tile, proj_width), BF16),
                        pltpu.VMEM((tile, y_width), BF16), pltpu.VMEM((tile, y_width), BF16)],
        compiler_params=_params("arbitrary"),
        name=name,
    )(h_bf16, w_proj, w_branch, *extra_inputs)


def _conv_branch_kernel(h_ref, w_ref, wa_ref, cw_ref, gb_ref, o_ref, tail_ref, p0_ref, p1_ref, y0_ref, y1_ref):
    tile = BRANCH_TILE
    s = pl.program_id(0)
    n_tiles = 2 * pl.num_programs(0)
    cw = cw_ref[...]

    @pl.when(s == 0)
    def _():
        p0_ref[...] = jnp.zeros_like(p0_ref)
        p1_ref[...] = jnp.zeros_like(p1_ref)

    def gated_input_row(p_ref, row):
        c = p_ref[row:row + 1, CONV_WIDTH:2 * CONV_WIDTH].astype(F32)
        return c * p_ref[row:row + 1, 2 * CONV_WIDTH:3 * CONV_WIDTH].astype(F32)

    def mix(p_ref, y_ref, out_ref, rows, tile_idx, y_prev, y_next):
        y = p_ref[:, CONV_WIDTH:2 * CONV_WIDTH].astype(F32) * p_ref[:, 2 * CONV_WIDTH:3 * CONV_WIDTH].astype(F32)
        y_prev = jnp.where(tile_idx > 0, y_prev, 0.0)
        y_next = jnp.where(tile_idx < n_tiles - 1, y_next, 0.0)
        row = lax.broadcasted_iota(jnp.int32, y.shape, 0)
        up = jnp.where(row == 0, y_prev, pltpu.roll(y, 1, 0))
        dn = jnp.where(row == tile - 1, y_next, pltpu.roll(y, tile - 1, 0))
        conv = cw[0:1, :] * up + cw[1:2, :] * y + cw[2:3, :] * dn
        z = p_ref[:, 3 * CONV_WIDTH:4 * CONV_WIDTH].astype(F32)
        y_ref[...] = (p_ref[:, :CONV_WIDTH].astype(F32) * conv * _silu(z)).astype(BF16)
        gate = _sigmoid(p_ref[:, 4 * CONV_WIDTH:].astype(F32) + gb_ref[...])
        out_ref[rows, :] = (gate * jnp.dot(y_ref[...], wa_ref[...], preferred_element_type=F32)).astype(BF16)

    lo, hi = slice(0, tile), slice(tile, 2 * tile)
    last_of_tile_before_p1 = gated_input_row(p0_ref, tile - 1)
    _project(h_ref, lo, w_ref, p0_ref)
    last_of_p1 = gated_input_row(p1_ref, tile - 1)
    mix(p1_ref, y1_ref, o_ref, lo, 2 * s - 1, last_of_tile_before_p1, gated_input_row(p0_ref, 0))
    _project(h_ref, hi, w_ref, p1_ref)
    mix(p0_ref, y0_ref, o_ref, hi, 2 * s, last_of_p1, gated_input_row(p1_ref, 0))

    @pl.when(s == pl.num_programs(0) - 1)
    def _():
        mix(p1_ref, y1_ref, tail_ref, slice(None), n_tiles - 1, gated_input_row(p0_ref, tile - 1),
            jnp.zeros((1, CONV_WIDTH), F32))


def _conv_branch(h_bf16, w_proj, w_branch, layer, conv_w, gate_bias):
    d = h_bf16.shape[1]
    return _branch_call(_conv_branch_kernel, "conv_branch", h_bf16, w_proj, w_branch, layer,
                        [conv_w, gate_bias.reshape(1, d)], [_resident((3, CONV_WIDTH)), _resident((1, d))],
                        CONV_WIDTH, d)


def _gmlp_branch_kernel(h_ref, w_ref, wb_ref, g_ref, b_ref, ws_ref, bias_ref, gb_ref, acc_ref, acc_tail_ref,
                        o_ref, tail_ref, p0_ref, p1_ref, y0_ref, y1_ref):
    tile = BRANCH_TILE

    @pl.when(pl.program_id(0) == 0)
    def _():
        p1_ref[...] = jnp.zeros_like(p1_ref)

    proj_width = p0_ref.shape[1]
    n_jobs = 2 + GMLP_GROUPS
    job_cols = proj_width // n_jobs

    def mix(p_ref, y_ref, out_ref, rows, add_ref, h_rows=None, q_ref=None):
        jobs = [slice(u * job_cols, (u + 1) * job_cols) for u in range(n_jobs)] if q_ref is not None else []

        def run_job():
            if q_ref is None:
                return
            c = jobs.pop(0)
            q_ref[:, c] = jnp.dot(h_ref[h_rows, :], w_ref[:, c], preferred_element_type=F32).astype(BF16)

        run_job()
        run_job()
        v = _layer_norm(_gelu_tanh(p_ref[:, GMLP_WIDTH:2 * GMLP_WIDTH].astype(F32)), g_ref[...], b_ref[...])
        v = v.astype(BF16)
        n_chunks = tile // CHUNK
        for g in range(GMLP_GROUPS):
            run_job()
            cols = slice(g * GMLP_GROUP_DIM, (g + 1) * GMLP_GROUP_DIM)
            zcols = slice(2 * GMLP_WIDTH + g * GMLP_GROUP_DIM, 2 * GMLP_WIDTH + (g + 1) * GMLP_GROUP_DIM)
            v_chunks = jnp.concatenate([v[c * CHUNK:(c + 1) * CHUNK, cols] for c in range(n_chunks)], axis=1)
            mixed_chunks = jnp.dot(ws_ref[g], v_chunks, preferred_element_type=F32)
            for c in range(n_chunks):
                r = slice(c * CHUNK, (c + 1) * CHUNK)
                mixed = mixed_chunks[:, c * GMLP_GROUP_DIM:(c + 1) * GMLP_GROUP_DIM] + bias_ref[:, cols]
                u = _gelu_tanh(p_ref[r, cols].astype(F32))
                y_ref[r, cols] = (u * mixed * _silu(p_ref[r, zcols].astype(F32))).astype(BF16)
        assert not jobs
        gate = _sigmoid(p_ref[:, 3 * GMLP_WIDTH:].astype(F32) + gb_ref[...])
        contrib = gate * jnp.dot(y_ref[...], wb_ref[...], preferred_element_type=F32)
        out_ref[rows, :] = (add_ref[rows, :].astype(F32) + contrib).astype(BF16)

    lo, hi = slice(0, tile), slice(tile, 2 * tile)
    mix(p1_ref, y1_ref, o_ref, lo, acc_ref, lo, p0_ref)
    mix(p0_ref, y0_ref, o_ref, hi, acc_ref, hi, p1_ref)

    @pl.when(pl.program_id(0) == pl.num_programs(0) - 1)
    def _():
        mix(p1_ref, y1_ref, tail_ref, slice(None), acc_tail_ref)


def _gmlp_branch(h_bf16, w_proj, w_branch, layer, ln_g, ln_b, spatial_w, spatial_b, gate_bias, accumulate_onto):
    d = h_bf16.shape[1]
    bias = jnp.repeat(spatial_b.T, GMLP_GROUP_DIM, axis=1)
    acc_main, acc_tail = accumulate_onto
    return _branch_call(
        _gmlp_branch_kernel, "gmlp_branch", h_bf16, w_proj, w_branch, layer,
        [ln_g.reshape(1, -1), ln_b.reshape(1, -1), spatial_w.astype(BF16), bias, gate_bias.reshape(1, d),
         acc_main, acc_tail],
        [_resident((1, GMLP_WIDTH)), _resident((1, GMLP_WIDTH)), _resident((GMLP_GROUPS, CHUNK, CHUNK)),
         _resident((CHUNK, GMLP_WIDTH)), _resident((1, d)),
         pl.BlockSpec((2 * BRANCH_TILE, d), lambda i: (i, 0)), _resident((BRANCH_TILE, d))],
        GMLP_WIDTH, d)


def _kv_pair_kernel(h0_ref, h1_ref, w_ref, cos_ref, sin_ref, o_ref):
    tile = BRANCH_TILE
    i = pl.program_id(0)
    for t, (h_ref, inside) in enumerate(((h0_ref, i > 0), (h1_ref, i < pl.num_programs(0) - 1))):
        rows = slice(t * tile, (t + 1) * tile)
        kv = jnp.dot(h_ref[...], w_ref[...], preferred_element_type=F32)
        cos, sin = cos_ref[rows, :], sin_ref[rows, :]
        parts = [_rope(kv[:, g * HEAD_DIM:(g + 1) * HEAD_DIM], cos, sin) for g in range(N_KV_HEADS)]
        out = jnp.concatenate(parts + [kv[:, KV_WIDTH:]], axis=1)
        o_ref[rows, :] = jnp.where(inside, out, 0.0).astype(BF16)


def _norm_kv_kernel(x_ref, g_ref, b_ref, w_ref, pos_ref, invf_ref, o_ref, hb_ref, cos_ref, sin_ref):
    h = _layer_norm(x_ref[...], g_ref[...], b_ref[...])
    hb = h.astype(BF16)
    hb_ref[...] = hb
    cos, sin = _rope_table(pos_ref[...], invf_ref[...])
    cos_ref[...] = cos
    sin_ref[...] = sin
    _kv_body(hb, w_ref, cos, sin, o_ref)


def _kv_body(h, w_ref, cos, sin, o_ref):
    i = pl.program_id(0)
    kv = jnp.dot(h, w_ref[...], preferred_element_type=F32)
    parts = [_rope(kv[:, g * HEAD_DIM:(g + 1) * HEAD_DIM], cos, sin) for g in range(N_KV_HEADS)]
    out = jnp.concatenate(parts + [kv[:, KV_WIDTH:]], axis=1)
    inside = (i > 0) & (i < pl.num_programs(0) - 1)
    o_ref[...] = jnp.where(inside, out, 0.0).astype(BF16)


def _kv_projection(h_bf16, w_kv, layer, cos_pad, sin_pad):
    s, d = h_bf16.shape
    tile = BRANCH_TILE
    n_tiles = s // tile
    n_pairs = n_tiles // 2
    tab_spec = pl.BlockSpec((2 * tile, HEAD_DIM), lambda i: (i, 0))
    return pl.pallas_call(
        _kv_pair_kernel,
        grid=(n_pairs + 1,),
        in_specs=[pl.BlockSpec((tile, d), lambda i: (jnp.maximum(2 * i - 1, 0), 0)),
                  pl.BlockSpec((tile, d), lambda i: (jnp.minimum(2 * i, n_tiles - 1), 0)),
                  _layer_resident(layer, (d, 2 * KV_WIDTH)), tab_spec, tab_spec],
        out_specs=pl.BlockSpec((2 * tile, 2 * KV_WIDTH), lambda i: (i, 0)),
        out_shape=jax.ShapeDtypeStruct(((n_pairs + 1) * 2 * tile, 2 * KV_WIDTH), BF16),
        compiler_params=_params("parallel"),
        name="kv_projection",
    )(h_bf16, h_bf16, w_kv, cos_pad, sin_pad)


def _norm_kv_projection(x, ln_g, ln_b, positions, w_kv, layer):
    s, d = x.shape
    tile = BRANCH_TILE
    n_tiles = s // tile
    padded = (n_tiles + 2) * tile
    positions_pad = jnp.pad(positions, (tile, tile)).reshape(padded, 1)
    tab_spec = pl.BlockSpec((tile, HEAD_DIM), lambda i: (i, 0))
    tab_shape = jax.ShapeDtypeStruct((padded, HEAD_DIM), F32)
    row_spec = pl.BlockSpec((tile, d), lambda i: (jnp.clip(i - 1, 0, n_tiles - 1), 0))
    return pl.pallas_call(
        _norm_kv_kernel,
        grid=(n_tiles + 2,),
        in_specs=[row_spec, _resident((1, d)), _resident((1, d)), _layer_resident(layer, (d, 2 * KV_WIDTH)),
                  pl.BlockSpec((tile, 1), lambda i: (i, 0)), _resident((1, HEAD_DIM))],
        out_specs=[pl.BlockSpec((tile, 2 * KV_WIDTH), lambda i: (i, 0)), row_spec, tab_spec, tab_spec],
        out_shape=[jax.ShapeDtypeStruct((padded, 2 * KV_WIDTH), BF16), jax.ShapeDtypeStruct((s, d), BF16),
                   tab_shape, tab_shape],
        compiler_params=_params("arbitrary"),
        name="norm_kv_projection",
    )(x, ln_g.reshape(1, d), ln_b.reshape(1, d), w_kv, positions_pad, _lane_frequencies())


def _attn_branch_kernel(sink_ref, h_ref, w_ref, wc_ref, kv_ref, kvp_ref, kvn_ref, cos_ref, sin_ref, o_ref,
                        p0_ref, p1_ref, y0_ref, y1_ref):
    tile = BRANCH_TILE
    n_blocks = tile // BLOCK
    s = pl.program_id(0)
    n_tiles = 2 * (pl.num_programs(0) - 1)
    scale = HEAD_DIM ** -0.5 * LOG2_E

    @pl.when(s == 0)
    def _():
        p1_ref[...] = jnp.zeros_like(p1_ref)

    q_row = lax.broadcasted_iota(jnp.int32, (GROUP * BLOCK, 3 * BLOCK), 0) % BLOCK
    k_col = lax.broadcasted_iota(jnp.int32, (GROUP * BLOCK, 3 * BLOCK), 1)
    band = (k_col >= q_row) & (k_col <= q_row + 2 * BLOCK)
    head_of_row = lax.broadcasted_iota(jnp.int32, (GROUP * BLOCK, 1), 0) // BLOCK

    def band_rows(w, b, cols):
        first = w * n_blocks + b
        if first == 0:
            return jnp.concatenate([kvp_ref[:, cols], kv_ref[:2 * BLOCK, cols]], axis=0)
        if first == 2 * n_blocks - 1:
            return jnp.concatenate([kv_ref[(first - 1) * BLOCK:, cols], kvn_ref[:, cols]], axis=0)
        return kv_ref[(first - 1) * BLOCK:(first + 2) * BLOCK, cols]

    units = [(g, b) for g in range(N_KV_HEADS) for b in range(n_blocks)]
    n_jobs = len(units) // UNITS_PER_JOB
    job_cols = 2 * ATTN_WIDTH // n_jobs
    out_cols = D_MODEL // n_jobs

    def attend(p_ref, y_ref, w, tile_idx, mxu_jobs):
        rows = slice(w * tile, (w + 1) * tile)
        cos_q = cos_ref[rows, :]
        sin_q = sin_ref[rows, :]
        q_heads = {}

        def scores_of(g, b):
            if g not in q_heads:
                q_heads[g] = []
                for j in range(GROUP):
                    c0 = (g * GROUP + j) * HEAD_DIM
                    q = p_ref[:, c0:c0 + HEAD_DIM].astype(F32)
                    q_heads[g].append((_rope(q, cos_q, sin_q) * scale).astype(BF16))
            kb = band_rows(w, b, slice(g * HEAD_DIM, (g + 1) * HEAD_DIM))
            q4 = jnp.concatenate([qh[b * BLOCK:(b + 1) * BLOCK] for qh in q_heads[g]], axis=0)
            return lax.dot_general(q4, kb, (((1,), (1,)), ((), ())), preferred_element_type=F32)

        scores = scores_of(*units[0])
        for u, (g, b) in enumerate(units):
            mxu_jobs[u]()
            r = slice(b * BLOCK, (b + 1) * BLOCK)
            sink_col = jnp.zeros((GROUP * BLOCK, 1), F32)
            for j in range(GROUP):
                sink_col = jnp.where(head_of_row == j, sink_ref[g * GROUP + j] * LOG2_E, sink_col)
            valid = band
            if b == 0:
                valid = valid & (k_col >= jnp.where(tile_idx == 0, BLOCK, 0))
            if b == n_blocks - 1:
                valid = valid & (k_col < jnp.where(tile_idx == n_tiles - 1, 2 * BLOCK, 3 * BLOCK))
            scores = jnp.where(valid, scores, -jnp.inf)
            m = jnp.maximum(jnp.max(scores, axis=-1, keepdims=True), sink_col)
            p = jnp.exp2(scores - m)
            denom = jnp.sum(p, axis=-1, keepdims=True) + jnp.exp2(sink_col - m)
            if u + 1 < len(units):
                scores = scores_of(*units[u + 1])
            vb = band_rows(w, b, slice(KV_WIDTH + g * HEAD_DIM, KV_WIDTH + (g + 1) * HEAD_DIM))
            out = jnp.dot(p.astype(BF16), vb, preferred_element_type=F32) / denom
            for j in range(GROUP):
                c0 = (g * GROUP + j) * HEAD_DIM
                zj = p_ref[r, ATTN_WIDTH + c0:ATTN_WIDTH + c0 + HEAD_DIM].astype(F32)
                y_ref[r, c0:c0 + HEAD_DIM] = (out[j * BLOCK:(j + 1) * BLOCK] * _silu(zj)).astype(BF16)

    def project_job(h_rows, p_ref, u):
        def job():
            c = slice(u * job_cols, (u + 1) * job_cols)
            p_ref[:, c] = jnp.dot(h_ref[h_rows, :], w_ref[:, c], preferred_element_type=F32).astype(BF16)
        return job

    def branch_job(y_ref, out_rows, u):
        def job():
            c = slice(u * out_cols, (u + 1) * out_cols)
            o_ref[out_rows, c] = jnp.dot(y_ref[...], wc_ref[:, c], preferred_element_type=F32).astype(BF16)
        return job

    def both(*jobs):
        def job():
            for j in jobs:
                j()
        return job

    def spread(jobs):
        return [jobs[u // UNITS_PER_JOB] if u % UNITS_PER_JOB == 0 else both() for u in range(len(units))]

    lo, hi = slice(0, tile), slice(tile, 2 * tile)
    attend(p1_ref, y1_ref, 0, 2 * s - 1, spread([project_job(lo, p0_ref, j) for j in range(n_jobs)]))
    attend(p0_ref, y0_ref, 1, 2 * s,
           spread([both(project_job(hi, p1_ref, j), branch_job(y1_ref, lo, j)) for j in range(n_jobs)]))
    for j in range(n_jobs):
        branch_job(y0_ref, hi, j)()


def _attention_branch(h_bf16, w_proj, w_branch, layer, sink, kv_pad, cos_pad, sin_pad):
    d = h_bf16.shape[1]
    tile = BRANCH_TILE
    halo_per_pair = 2 * tile // BLOCK
    kv_width = 2 * KV_WIDTH
    n_halo = kv_pad.shape[0] // BLOCK
    tab_spec = pl.BlockSpec((2 * tile, HEAD_DIM), lambda i: (i, 0))
    specs = [pl.BlockSpec((2 * tile, kv_width), lambda i: (i, 0)),
             pl.BlockSpec((BLOCK, kv_width), lambda i: (jnp.maximum(i * halo_per_pair - 1, 0), 0)),
             pl.BlockSpec((BLOCK, kv_width), lambda i: (jnp.minimum((i + 1) * halo_per_pair, n_halo - 1), 0)),
             tab_spec, tab_spec]
    s, _ = h_bf16.shape
    n_pairs = s // (2 * tile)
    return pl.pallas_call(
        _attn_branch_kernel,
        grid=(n_pairs + 1,),
        in_specs=[pl.BlockSpec(memory_space=pltpu.SMEM),
                  pl.BlockSpec((2 * tile, d), lambda i: (jnp.minimum(i, n_pairs - 1), 0)),
                  _layer_resident(layer, (d, 2 * ATTN_WIDTH)), _layer_resident(layer, (ATTN_WIDTH, d))] + specs,
        out_specs=pl.BlockSpec((2 * tile, d), lambda i: (i, 0)),
        out_shape=jax.ShapeDtypeStruct(((n_pairs + 1) * 2 * tile, d), BF16),
        scratch_shapes=[pltpu.VMEM((tile, 2 * ATTN_WIDTH), BF16), pltpu.VMEM((tile, 2 * ATTN_WIDTH), BF16),
                        pltpu.VMEM((tile, ATTN_WIDTH), BF16), pltpu.VMEM((tile, ATTN_WIDTH), BF16)],
        compiler_params=_params("arbitrary"),
        name="attention_branch",
    )(sink, h_bf16, w_proj, w_branch, kv_pad, kv_pad, kv_pad, cos_pad, sin_pad)


def _final_kernel(n_norm_inputs, cab0_ref, cab1_ref, cab_tail_ref, cc0_ref, cc1_ref, hb_ref, h_ref, wr_ref, wo_ref,
                  gb_ref, g_ref, b_ref, *refs):
    norm_refs, out_refs = refs[:n_norm_inputs], refs[n_norm_inputs:]
    tile = BRANCH_TILE
    is_last = pl.program_id(0) == pl.num_programs(0) - 1
    for t, (cab_ref, cc_ref) in enumerate(((cab0_ref, cc0_ref), (cab1_ref, cc1_ref))):
        rows = slice(t * tile, (t + 1) * tile)
        cab, cc = cab_ref[...], cc_ref[...]
        if t == 1:
            cab = jnp.where(is_last, cab_tail_ref[...], cab)
        gate_c = _sigmoid(jnp.dot(hb_ref[rows, :], wr_ref[...], preferred_element_type=F32) + gb_ref[...])
        merged = cab.astype(F32) + gate_c * cc.astype(F32)
        out = jnp.dot(merged.astype(BF16), wo_ref[...], preferred_element_type=F32)
        h = h_ref[rows, :]
        if norm_refs:
            h = _layer_norm(h, norm_refs[0][...], norm_refs[1][...])
        y = _layer_norm(ALPHA * h + out, g_ref[...], b_ref[...])
        out_refs[0][rows, :] = y
        if len(out_refs) > 1:
            out_refs[1][rows, :] = y.astype(BF16)


def _merge_project_norm(contrib_ab, contrib_c, h_bf16, h, w_gate_c, w_o, layer, gate_bias_c, ln_g, ln_b, emit_bf16,
                        residual_norm=()):
    s, d = h.shape
    tile = BRANCH_TILE
    n_tiles = s // tile
    row = pl.BlockSpec((2 * tile, d), lambda i: (i, 0))
    first = pl.BlockSpec((tile, d), lambda i: (2 * i + 1, 0))
    second = pl.BlockSpec((tile, d), lambda i: (2 * i + 2, 0))
    second_clamped = pl.BlockSpec((tile, d), lambda i: (jnp.minimum(2 * i + 2, n_tiles - 1), 0))
    out_specs = [row, row] if emit_bf16 else [row]
    out_shape = [jax.ShapeDtypeStruct((s, d), F32)] + ([jax.ShapeDtypeStruct((s, d), BF16)] if emit_bf16 else [])
    vectors = [gate_bias_c, ln_g, ln_b, *residual_norm]
    return pl.pallas_call(
        functools.partial(_final_kernel, len(residual_norm)),
        grid=(s // (2 * tile),),
        in_specs=[first, second_clamped, _resident((tile, d)), first, second, row, row,
                  _layer_resident(layer, (d, d)), _layer_resident(layer, (d, d))]
        + [_resident((1, d)) for _ in vectors],
        out_specs=out_specs,
        out_shape=out_shape,
        compiler_params=_params("parallel"),
        name="merge_project_norm",
    )(contrib_ab[0], contrib_ab[0], contrib_ab[1], contrib_c, contrib_c, h_bf16, h, w_gate_c, w_o,
      *[v.reshape(1, d) for v in vectors])


def kernel(x, positions, ln0_g, ln0_b, w_in, conv_w, gmlp_ln_g, gmlp_ln_b, spatial_w, spatial_b, sink,
           w_branch_a, w_branch_b, w_branch_c, gate_b, w_out, ln_g, ln_b):
    bsz, s, d = x.shape
    assert (bsz, d) == (1, D_MODEL) and s % 1024 == 0 and w_in.shape == (DEPTH, D_MODEL, IN_WIDTH)

    def tiles(off, width):
        return tuple(range(off // WEIGHT_TILE, (off + width) // WEIGHT_TILE))

    w_conv = _gather_cast_columns(w_in, tiles(OFF_CONV, 4 * CONV_WIDTH) + tiles(OFF_GATE_A, d))
    w_gmlp = _gather_cast_columns(w_in, tiles(OFF_GMLP, 3 * GMLP_WIDTH) + tiles(OFF_GATE_B, d))
    w_qz = _gather_cast_columns(w_in, tiles(OFF_Q, ATTN_WIDTH) + tiles(OFF_ATTN_Z, ATTN_WIDTH))
    w_kv = _gather_cast_columns(w_in, tiles(OFF_KV, 2 * KV_WIDTH))
    w_gate_c = _gather_cast_columns(w_in, tiles(OFF_GATE_C, d))
    wa, wb, wc, wo = (_cast_bf16(w) for w in (w_branch_a, w_branch_b, w_branch_c, w_out))

    h = h_bf16 = cos_pad = sin_pad = None
    for l in range(DEPTH):
        if l == 0:
            h = x.reshape(s, d)
            kv_pad, h_bf16, cos_pad, sin_pad = _norm_kv_projection(h, ln0_g, ln0_b, positions.reshape(s), w_kv, l)
        else:
            kv_pad = _kv_projection(h_bf16, w_kv, l, cos_pad, sin_pad)
        contrib_a = _conv_branch(h_bf16, w_conv, wa, l, conv_w[l], gate_b[l, 0])
        contrib_ab = _gmlp_branch(h_bf16, w_gmlp, wb, l, gmlp_ln_g[l], gmlp_ln_b[l],
                                  spatial_w[l], spatial_b[l], gate_b[l, 1], accumulate_onto=contrib_a)
        contrib_c = _attention_branch(h_bf16, w_qz, wc, l, sink[l], kv_pad, cos_pad, sin_pad)
        outs = _merge_project_norm(contrib_ab, contrib_c, h_bf16, h, w_gate_c, wo, l, gate_b[l, 2],
                                   ln_g[l], ln_b[l], emit_bf16=l + 1 < DEPTH,
                                   residual_norm=(ln0_g, ln0_b) if l == 0 else ())
        h = outs[0]
        h_bf16 = outs[1] if l + 1 < DEPTH else None
    return h.reshape(bsz, s, d)
```

```python
import jax
import jax.numpy as jnp
from jax import lax
from jax.experimental import pallas as pl
from jax.experimental.pallas import tpu as pltpu

D_MODEL = 2048
DEPTH = 2
HEAD_DIM = 128
N_Q_HEADS = 16
N_KV_HEADS = 4
GROUP = N_Q_HEADS // N_KV_HEADS
ATTN_WIDTH = N_Q_HEADS * HEAD_DIM
KV_WIDTH = N_KV_HEADS * HEAD_DIM
CONV_WIDTH = 1024
GMLP_WIDTH = 1024
GMLP_GROUPS = 8
GMLP_GROUP_DIM = GMLP_WIDTH // GMLP_GROUPS
CHUNK = 128
BLOCK = 128
ROPE_THETA = 500000.0
ROPE_DIM = HEAD_DIM // 4
ROPE_HALF = ROPE_DIM // 2
LN_EPS = 1e-5
LOG2_E = 1.4426950408889634
ALPHA = (2.0 * DEPTH) ** 0.25
IN_WIDTH = 4 * CONV_WIDTH + 3 * GMLP_WIDTH + 2 * ATTN_WIDTH + 2 * KV_WIDTH + 3 * D_MODEL

OFF_CONV, OFF_GMLP, OFF_Q, OFF_KV, OFF_ATTN_Z = 0, 4096, 7168, 9216, 10240
OFF_GATE_A, OFF_GATE_B, OFF_GATE_C = 12288, 14336, 16384

WEIGHT_TILE = 1024
PROJ_CHUNK = 1024
BRANCH_TILE = 256
UNITS_PER_JOB = 2
VMEM_LIMIT_BYTES = 56 * 1024 * 1024

F32 = jnp.float32
BF16 = jnp.bfloat16


def _params(*semantics):
    return pltpu.CompilerParams(dimension_semantics=semantics, vmem_limit_bytes=VMEM_LIMIT_BYTES)


def _silu(x):
    return x / (1.0 + jnp.exp(-x))


def _sigmoid(x):
    return 1.0 / (1.0 + jnp.exp(-x))


def _gelu_tanh(x):
    return 0.5 * x * (1.0 + jnp.tanh(0.7978845608028654 * (x + 0.044715 * (x * x * x))))


def _layer_norm(x, g, b):
    mu = jnp.mean(x, axis=-1, keepdims=True)
    xc = x - mu
    var = jnp.mean(xc * xc, axis=-1, keepdims=True)
    return xc * lax.rsqrt(var + LN_EPS) * g + b


def _resident(shape):
    return pl.BlockSpec(shape, lambda i: (0,) * len(shape), pipeline_mode=pl.Buffered(1))


def _layer_resident(layer, shape):
    return pl.BlockSpec((None,) + shape, lambda i: (layer,) + (0,) * len(shape), pipeline_mode=pl.Buffered(1))


def _rope_table(pos, invf):
    ang = pos.astype(F32) * invf
    lane = lax.broadcasted_iota(jnp.int32, ang.shape, 1)
    c = jnp.cos(ang)
    s = jnp.sin(ang)
    return jnp.where(lane < ROPE_DIM, c, 1.0), jnp.where(lane < ROPE_HALF, -s, jnp.where(lane < ROPE_DIM, s, 0.0))


def _lane_frequencies():
    inv_freq = ROPE_THETA ** (-jnp.arange(ROPE_HALF, dtype=F32) / ROPE_HALF)
    return jnp.tile(inv_freq, HEAD_DIM // ROPE_HALF).reshape(1, HEAD_DIM)


def _rope(t, cos, sin):
    lane = lax.broadcasted_iota(jnp.int32, t.shape, 1)
    partner = jnp.where(lane < ROPE_HALF, pltpu.roll(t, HEAD_DIM - ROPE_HALF, 1), pltpu.roll(t, ROPE_HALF, 1))
    return t * cos + partner * sin


def _cast_kernel(tiles_ref, w_ref, o_ref):
    del tiles_ref
    o_ref[...] = w_ref[...].astype(o_ref.dtype)


def _gather_cast_columns(w, col_tiles, tile=WEIGHT_TILE):
    layers, k, _ = w.shape
    table = jnp.asarray(col_tiles, jnp.int32)
    return pl.pallas_call(
        _cast_kernel,
        grid_spec=pltpu.PrefetchScalarGridSpec(
            num_scalar_prefetch=1,
            grid=(layers, len(col_tiles)),
            in_specs=[pl.BlockSpec((None, k, tile), lambda l, j, t: (l, 0, t[j]))],
            out_specs=pl.BlockSpec((None, k, tile), lambda l, j, t: (l, 0, j)),
        ),
        out_shape=jax.ShapeDtypeStruct((layers, k, len(col_tiles) * tile), BF16),
        compiler_params=_params("parallel", "parallel"),
        name="gather_cast_columns",
    )(table, w)


def _cast_bf16(w):
    return _gather_cast_columns(w, tuple(range(w.shape[-1] // WEIGHT_TILE)))


def _project(h_ref, rows, w_ref, p_ref):
    h = h_ref[rows, :]
    for c in range(0, p_ref.shape[1], PROJ_CHUNK):
        p_ref[:, c:c + PROJ_CHUNK] = jnp.dot(
            h, w_ref[:, c:c + PROJ_CHUNK], preferred_element_type=F32).astype(BF16)


def _branch_call(body, name, h_bf16, w_proj, w_branch, layer, extra_inputs, extra_specs, y_width, out_width):
    s, d = h_bf16.shape
    tile = BRANCH_TILE
    n_pairs = s // (2 * tile)
    proj_width = w_proj.shape[2]
    return pl.pallas_call(
        body,
        grid=(n_pairs,),
        in_specs=[pl.BlockSpec((2 * tile, d), lambda i: (i, 0)),
                  _layer_resident(layer, (d, proj_width)), _layer_resident(layer, w_branch.shape[1:])] + extra_specs,
        out_specs=[pl.BlockSpec((2 * tile, out_width), lambda i: (i, 0)),
                   pl.BlockSpec((tile, out_width), lambda i: (0, 0))],
        out_shape=[jax.ShapeDtypeStruct((s, out_width), BF16), jax.ShapeDtypeStruct((tile, out_width), BF16)],
        scratch_shapes=[pltpu.VMEM((tile, proj_width), BF16), pltpu.VMEM((tile, proj_width), BF16),
                        pltpu.VMEM((tile, y_width), BF16), pltpu.VMEM((tile, y_width), BF16)],
        compiler_params=_params("arbitrary"),
        name=name,
    )(h_bf16, w_proj, w_branch, *extra_inputs)


def _conv_branch_kernel(h_ref, w_ref, wa_ref, cw_ref, gb_ref, o_ref, tail_ref, p0_ref, p1_ref, y0_ref, y1_ref):
    tile = BRANCH_TILE
    s = pl.program_id(0)
    n_tiles = 2 * pl.num_programs(0)
    cw = cw_ref[...]

    @pl.when(s == 0)
    def _():
        p0_ref[...] = jnp.zeros_like(p0_ref)
        p1_ref[...] = jnp.zeros_like(p1_ref)

    def gated_input_row(p_ref, row):
        c = p_ref[row:row + 1, CONV_WIDTH:2 * CONV_WIDTH].astype(F32)
        return c * p_ref[row:row + 1, 2 * CONV_WIDTH:3 * CONV_WIDTH].astype(F32)

    def mix(p_ref, y_ref, out_ref, rows, tile_idx, y_prev, y_next):
        y = p_ref[:, CONV_WIDTH:2 * CONV_WIDTH].astype(F32) * p_ref[:, 2 * CONV_WIDTH:3 * CONV_WIDTH].astype(F32)
        y_prev = jnp.where(tile_idx > 0, y_prev, 0.0)
        y_next = jnp.where(tile_idx < n_tiles - 1, y_next, 0.0)
        row = lax.broadcasted_iota(jnp.int32, y.shape, 0)
        up = jnp.where(row == 0, y_prev, pltpu.roll(y, 1, 0))
        dn = jnp.where(row == tile - 1, y_next, pltpu.roll(y, tile - 1, 0))
        conv = cw[0:1, :] * up + cw[1:2, :] * y + cw[2:3, :] * dn
        z = p_ref[:, 3 * CONV_WIDTH:4 * CONV_WIDTH].astype(F32)
        y_ref[...] = (p_ref[:, :CONV_WIDTH].astype(F32) * conv * _silu(z)).astype(BF16)
        gate = _sigmoid(p_ref[:, 4 * CONV_WIDTH:].astype(F32) + gb_ref[...])
        out_ref[rows, :] = (gate * jnp.dot(y_ref[...], wa_ref[...], preferred_element_type=F32)).astype(BF16)

    lo, hi = slice(0, tile), slice(tile, 2 * tile)
    last_of_tile_before_p1 = gated_input_row(p0_ref, tile - 1)
    _project(h_ref, lo, w_ref, p0_ref)
    last_of_p1 = gated_input_row(p1_ref, tile - 1)
    mix(p1_ref, y1_ref, o_ref, lo, 2 * s - 1, last_of_tile_before_p1, gated_input_row(p0_ref, 0))
    _project(h_ref, hi, w_ref, p1_ref)
    mix(p0_ref, y0_ref, o_ref, hi, 2 * s, last_of_p1, gated_input_row(p1_ref, 0))

    @pl.when(s == pl.num_programs(0) - 1)
    def _():
        mix(p1_ref, y1_ref, tail_ref, slice(None), n_tiles - 1, gated_input_row(p0_ref, tile - 1),
            jnp.zeros((1, CONV_WIDTH), F32))


def _conv_branch(h_bf16, w_proj, w_branch, layer, conv_w, gate_bias):
    d = h_bf16.shape[1]
    return _branch_call(_conv_branch_kernel, "conv_branch", h_bf16, w_proj, w_branch, layer,
                        [conv_w, gate_bias.reshape(1, d)], [_resident((3, CONV_WIDTH)), _resident((1, d))],
                        CONV_WIDTH, d)


def _gmlp_branch_kernel(h_ref, w_ref, wb_ref, g_ref, b_ref, ws_ref, bias_ref, gb_ref, acc_ref, acc_tail_ref,
                        o_ref, tail_ref, p0_ref, p1_ref, y0_ref, y1_ref):
    tile = BRANCH_TILE

    @pl.when(pl.program_id(0) == 0)
    def _():
        p1_ref[...] = jnp.zeros_like(p1_ref)

    proj_width = p0_ref.shape[1]
    n_jobs = 1 + GMLP_GROUPS // 2
    job_cols = proj_width // n_jobs

    def mix(p_ref, y_ref, out_ref, rows, add_ref, h_rows=None, q_ref=None):
        jobs = [slice(u * job_cols, (u + 1) * job_cols) for u in range(n_jobs)] if q_ref is not None else []

        def run_job():
            if q_ref is None:
                return
            c = jobs.pop(0)
            q_ref[:, c] = jnp.dot(h_ref[h_rows, :], w_ref[:, c], preferred_element_type=F32).astype(BF16)

        run_job()
        v = _layer_norm(_gelu_tanh(p_ref[:, GMLP_WIDTH:2 * GMLP_WIDTH].astype(F32)), g_ref[...], b_ref[...])
        v = v.astype(BF16)
        n_chunks = tile // CHUNK
        for g in range(GMLP_GROUPS):
            if g % 2 == 0:
                run_job()
            cols = slice(g * GMLP_GROUP_DIM, (g + 1) * GMLP_GROUP_DIM)
            zcols = slice(2 * GMLP_WIDTH + g * GMLP_GROUP_DIM, 2 * GMLP_WIDTH + (g + 1) * GMLP_GROUP_DIM)
            v_chunks = jnp.concatenate([v[c * CHUNK:(c + 1) * CHUNK, cols] for c in range(n_chunks)], axis=1)
            mixed_chunks = jnp.dot(ws_ref[g], v_chunks, preferred_element_type=F32)
            for c in range(n_chunks):
                r = slice(c * CHUNK, (c + 1) * CHUNK)
                mixed = mixed_chunks[:, c * GMLP_GROUP_DIM:(c + 1) * GMLP_GROUP_DIM] + bias_ref[:, cols]
                u = _gelu_tanh(p_ref[r, cols].astype(F32))
                y_ref[r, cols] = (u * mixed * _silu(p_ref[r, zcols].astype(F32))).astype(BF16)
        assert not jobs
        gate = _sigmoid(p_ref[:, 3 * GMLP_WIDTH:].astype(F32) + gb_ref[...])
        contrib = gate * jnp.dot(y_ref[...], wb_ref[...], preferred_element_type=F32)
        out_ref[rows, :] = (add_ref[rows, :].astype(F32) + contrib).astype(BF16)

    lo, hi = slice(0, tile), slice(tile, 2 * tile)
    mix(p1_ref, y1_ref, o_ref, lo, acc_ref, lo, p0_ref)
    mix(p0_ref, y0_ref, o_ref, hi, acc_ref, hi, p1_ref)

    @pl.when(pl.program_id(0) == pl.num_programs(0) - 1)
    def _():
        mix(p1_ref, y1_ref, tail_ref, slice(None), acc_tail_ref)


def _gmlp_branch(h_bf16, w_proj, w_branch, layer, ln_g, ln_b, spatial_w, spatial_b, gate_bias, accumulate_onto):
    d = h_bf16.shape[1]
    bias = jnp.repeat(spatial_b.T, GMLP_GROUP_DIM, axis=1)
    acc_main, acc_tail = accumulate_onto
    return _branch_call(
        _gmlp_branch_kernel, "gmlp_branch", h_bf16, w_proj, w_branch, layer,
        [ln_g.reshape(1, -1), ln_b.reshape(1, -1), spatial_w.astype(BF16), bias, gate_bias.reshape(1, d),
         acc_main, acc_tail],
        [_resident((1, GMLP_WIDTH)), _resident((1, GMLP_WIDTH)), _resident((GMLP_GROUPS, CHUNK, CHUNK)),
         _resident((CHUNK, GMLP_WIDTH)), _resident((1, d)),
         pl.BlockSpec((2 * BRANCH_TILE, d), lambda i: (i, 0)), _resident((BRANCH_TILE, d))],
        GMLP_WIDTH, d)


def _kv_pair_kernel(h0_ref, h1_ref, w_ref, cos_ref, sin_ref, o_ref):
    tile = BRANCH_TILE
    i = pl.program_id(0)
    for t, (h_ref, inside) in enumerate(((h0_ref, i > 0), (h1_ref, i < pl.num_programs(0) - 1))):
        rows = slice(t * tile, (t + 1) * tile)
        kv = jnp.dot(h_ref[...], w_ref[...], preferred_element_type=F32)
        cos, sin = cos_ref[rows, :], sin_ref[rows, :]
        parts = [_rope(kv[:, g * HEAD_DIM:(g + 1) * HEAD_DIM], cos, sin) for g in range(N_KV_HEADS)]
        out = jnp.concatenate(parts + [kv[:, KV_WIDTH:]], axis=1)
        o_ref[rows, :] = jnp.where(inside, out, 0.0).astype(BF16)


def _norm_kv_kernel(x_ref, g_ref, b_ref, w_ref, pos_ref, invf_ref, o_ref, hf_ref, hb_ref, cos_ref, sin_ref):
    h = _layer_norm(x_ref[...], g_ref[...], b_ref[...])
    hf_ref[...] = h
    hb = h.astype(BF16)
    hb_ref[...] = hb
    cos, sin = _rope_table(pos_ref[...], invf_ref[...])
    cos_ref[...] = cos
    sin_ref[...] = sin
    _kv_body(hb, w_ref, cos, sin, o_ref)


def _kv_body(h, w_ref, cos, sin, o_ref):
    i = pl.program_id(0)
    kv = jnp.dot(h, w_ref[...], preferred_element_type=F32)
    parts = [_rope(kv[:, g * HEAD_DIM:(g + 1) * HEAD_DIM], cos, sin) for g in range(N_KV_HEADS)]
    out = jnp.concatenate(parts + [kv[:, KV_WIDTH:]], axis=1)
    inside = (i > 0) & (i < pl.num_programs(0) - 1)
    o_ref[...] = jnp.where(inside, out, 0.0).astype(BF16)


def _kv_projection(h_bf16, w_kv, layer, cos_pad, sin_pad):
    s, d = h_bf16.shape
    tile = BRANCH_TILE
    n_tiles = s // tile
    n_pairs = n_tiles // 2
    tab_spec = pl.BlockSpec((2 * tile, HEAD_DIM), lambda i: (i, 0))
    return pl.pallas_call(
        _kv_pair_kernel,
        grid=(n_pairs + 1,),
        in_specs=[pl.BlockSpec((tile, d), lambda i: (jnp.maximum(2 * i - 1, 0), 0)),
                  pl.BlockSpec((tile, d), lambda i: (jnp.minimum(2 * i, n_tiles - 1), 0)),
                  _layer_resident(layer, (d, 2 * KV_WIDTH)), tab_spec, tab_spec],
        out_specs=pl.BlockSpec((2 * tile, 2 * KV_WIDTH), lambda i: (i, 0)),
        out_shape=jax.ShapeDtypeStruct(((n_pairs + 1) * 2 * tile, 2 * KV_WIDTH), BF16),
        compiler_params=_params("parallel"),
        name="kv_projection",
    )(h_bf16, h_bf16, w_kv, cos_pad, sin_pad)


def _norm_kv_projection(x, ln_g, ln_b, positions, w_kv, layer):
    s, d = x.shape
    tile = BRANCH_TILE
    n_tiles = s // tile
    padded = (n_tiles + 2) * tile
    positions_pad = jnp.pad(positions, (tile, tile)).reshape(padded, 1)
    tab_spec = pl.BlockSpec((tile, HEAD_DIM), lambda i: (i, 0))
    tab_shape = jax.ShapeDtypeStruct((padded, HEAD_DIM), F32)
    row_spec = pl.BlockSpec((tile, d), lambda i: (jnp.clip(i - 1, 0, n_tiles - 1), 0))
    return pl.pallas_call(
        _norm_kv_kernel,
        grid=(n_tiles + 2,),
        in_specs=[row_spec, _resident((1, d)), _resident((1, d)), _layer_resident(layer, (d, 2 * KV_WIDTH)),
                  pl.BlockSpec((tile, 1), lambda i: (i, 0)), _resident((1, HEAD_DIM))],
        out_specs=[pl.BlockSpec((tile, 2 * KV_WIDTH), lambda i: (i, 0)), row_spec, row_spec, tab_spec, tab_spec],
        out_shape=[jax.ShapeDtypeStruct((padded, 2 * KV_WIDTH), BF16), jax.ShapeDtypeStruct((s, d), F32),
                   jax.ShapeDtypeStruct((s, d), BF16), tab_shape, tab_shape],
        compiler_params=_params("arbitrary"),
        name="norm_kv_projection",
    )(x, ln_g.reshape(1, d), ln_b.reshape(1, d), w_kv, positions_pad, _lane_frequencies())


def _attn_branch_kernel(sink_ref, h_ref, w_ref, wc_ref, kv_ref, kvp_ref, kvn_ref, cos_ref, sin_ref, o_ref,
                        p0_ref, p1_ref, y0_ref, y1_ref):
    tile = BRANCH_TILE
    n_blocks = tile // BLOCK
    s = pl.program_id(0)
    n_tiles = 2 * (pl.num_programs(0) - 1)
    scale = HEAD_DIM ** -0.5 * LOG2_E

    @pl.when(s == 0)
    def _():
        p1_ref[...] = jnp.zeros_like(p1_ref)

    q_row = lax.broadcasted_iota(jnp.int32, (GROUP * BLOCK, 3 * BLOCK), 0) % BLOCK
    k_col = lax.broadcasted_iota(jnp.int32, (GROUP * BLOCK, 3 * BLOCK), 1)
    band = (k_col >= q_row) & (k_col <= q_row + 2 * BLOCK)
    head_of_row = lax.broadcasted_iota(jnp.int32, (GROUP * BLOCK, 1), 0) // BLOCK

    def band_rows(w, b, cols):
        first = w * n_blocks + b
        if first == 0:
            return jnp.concatenate([kvp_ref[:, cols], kv_ref[:2 * BLOCK, cols]], axis=0)
        if first == 2 * n_blocks - 1:
            return jnp.concatenate([kv_ref[(first - 1) * BLOCK:, cols], kvn_ref[:, cols]], axis=0)
        return kv_ref[(first - 1) * BLOCK:(first + 2) * BLOCK, cols]

    units = [(g, b) for g in range(N_KV_HEADS) for b in range(n_blocks)]
    n_jobs = len(units) // UNITS_PER_JOB
    job_cols = 2 * ATTN_WIDTH // n_jobs
    out_cols = D_MODEL // n_jobs

    def attend(p_ref, y_ref, w, tile_idx, mxu_jobs):
        rows = slice(w * tile, (w + 1) * tile)
        cos_q = cos_ref[rows, :]
        sin_q = sin_ref[rows, :]
        q_heads = {}

        def scores_of(g, b):
            if g not in q_heads:
                q_heads[g] = []
                for j in range(GROUP):
                    c0 = (g * GROUP + j) * HEAD_DIM
                    q = p_ref[:, c0:c0 + HEAD_DIM].astype(F32)
                    q_heads[g].append((_rope(q, cos_q, sin_q) * scale).astype(BF16))
            kb = band_rows(w, b, slice(g * HEAD_DIM, (g + 1) * HEAD_DIM))
            q4 = jnp.concatenate([qh[b * BLOCK:(b + 1) * BLOCK] for qh in q_heads[g]], axis=0)
            return lax.dot_general(q4, kb, (((1,), (1,)), ((), ())), preferred_element_type=F32)

        scores = scores_of(*units[0])
        for u, (g, b) in enumerate(units):
            mxu_jobs[u]()
            r = slice(b * BLOCK, (b + 1) * BLOCK)
            sink_col = jnp.zeros((GROUP * BLOCK, 1), F32)
            for j in range(GROUP):
                sink_col = jnp.where(head_of_row == j, sink_ref[g * GROUP + j] * LOG2_E, sink_col)
            valid = band
            if b == 0:
                valid = valid & (k_col >= jnp.where(tile_idx == 0, BLOCK, 0))
            if b == n_blocks - 1:
                valid = valid & (k_col < jnp.where(tile_idx == n_tiles - 1, 2 * BLOCK, 3 * BLOCK))
            scores = jnp.where(valid, scores, -jnp.inf)
            m = jnp.maximum(jnp.max(scores, axis=-1, keepdims=True), sink_col)
            p = jnp.exp2(scores - m)
            denom = jnp.sum(p, axis=-1, keepdims=True) + jnp.exp2(sink_col - m)
            if u + 1 < len(units):
                scores = scores_of(*units[u + 1])
            vb = band_rows(w, b, slice(KV_WIDTH + g * HEAD_DIM, KV_WIDTH + (g + 1) * HEAD_DIM))
            out = jnp.dot(p.astype(BF16), vb, preferred_element_type=F32) / denom
            for j in range(GROUP):
                c0 = (g * GROUP + j) * HEAD_DIM
                zj = p_ref[r, ATTN_WIDTH + c0:ATTN_WIDTH + c0 + HEAD_DIM].astype(F32)
                y_ref[r, c0:c0 + HEAD_DIM] = (out[j * BLOCK:(j + 1) * BLOCK] * _silu(zj)).astype(BF16)

    def project_job(h_rows, p_ref, u):
        def job():
            c = slice(u * job_cols, (u + 1) * job_cols)
            p_ref[:, c] = jnp.dot(h_ref[h_rows, :], w_ref[:, c], preferred_element_type=F32).astype(BF16)
        return job

    def branch_job(y_ref, out_rows, u):
        def job():
            c = slice(u * out_cols, (u + 1) * out_cols)
            o_ref[out_rows, c] = jnp.dot(y_ref[...], wc_ref[:, c], preferred_element_type=F32).astype(BF16)
        return job

    def both(*jobs):
        def job():
            for j in jobs:
                j()
        return job

    def spread(jobs):
        return [jobs[u // UNITS_PER_JOB] if u % UNITS_PER_JOB == 0 else both() for u in range(len(units))]

    lo, hi = slice(0, tile), slice(tile, 2 * tile)
    attend(p1_ref, y1_ref, 0, 2 * s - 1, spread([project_job(lo, p0_ref, j) for j in range(n_jobs)]))
    attend(p0_ref, y0_ref, 1, 2 * s,
           spread([both(project_job(hi, p1_ref, j), branch_job(y1_ref, lo, j)) for j in range(n_jobs)]))
    for j in range(n_jobs):
        branch_job(y0_ref, hi, j)()


def _attention_branch(h_bf16, w_proj, w_branch, layer, sink, kv_pad, cos_pad, sin_pad):
    d = h_bf16.shape[1]
    tile = BRANCH_TILE
    halo_per_pair = 2 * tile // BLOCK
    kv_width = 2 * KV_WIDTH
    n_halo = kv_pad.shape[0] // BLOCK
    tab_spec = pl.BlockSpec((2 * tile, HEAD_DIM), lambda i: (i, 0))
    specs = [pl.BlockSpec((2 * tile, kv_width), lambda i: (i, 0)),
             pl.BlockSpec((BLOCK, kv_width), lambda i: (jnp.maximum(i * halo_per_pair - 1, 0), 0)),
             pl.BlockSpec((BLOCK, kv_width), lambda i: (jnp.minimum((i + 1) * halo_per_pair, n_halo - 1), 0)),
             tab_spec, tab_spec]
    s, _ = h_bf16.shape
    n_pairs = s // (2 * tile)
    return pl.pallas_call(
        _attn_branch_kernel,
        grid=(n_pairs + 1,),
        in_specs=[pl.BlockSpec(memory_space=pltpu.SMEM),
                  pl.BlockSpec((2 * tile, d), lambda i: (jnp.minimum(i, n_pairs - 1), 0)),
                  _layer_resident(layer, (d, 2 * ATTN_WIDTH)), _layer_resident(layer, (ATTN_WIDTH, d))] + specs,
        out_specs=pl.BlockSpec((2 * tile, d), lambda i: (i, 0)),
        out_shape=jax.ShapeDtypeStruct(((n_pairs + 1) * 2 * tile, d), BF16),
        scratch_shapes=[pltpu.VMEM((tile, 2 * ATTN_WIDTH), BF16), pltpu.VMEM((tile, 2 * ATTN_WIDTH), BF16),
                        pltpu.VMEM((tile, ATTN_WIDTH), BF16), pltpu.VMEM((tile, ATTN_WIDTH), BF16)],
        compiler_params=_params("arbitrary"),
        name="attention_branch",
    )(sink, h_bf16, w_proj, w_branch, kv_pad, kv_pad, kv_pad, cos_pad, sin_pad)


def _final_kernel(cab0_ref, cab1_ref, cab_tail_ref, cc0_ref, cc1_ref, hb_ref, h_ref, wr_ref, wo_ref,
                  gb_ref, g_ref, b_ref, *out_refs):
    tile = BRANCH_TILE
    is_last = pl.program_id(0) == pl.num_programs(0) - 1
    for t, (cab_ref, cc_ref) in enumerate(((cab0_ref, cc0_ref), (cab1_ref, cc1_ref))):
        rows = slice(t * tile, (t + 1) * tile)
        cab, cc = cab_ref[...], cc_ref[...]
        if t == 1:
            cab = jnp.where(is_last, cab_tail_ref[...], cab)
        gate_c = _sigmoid(jnp.dot(hb_ref[rows, :], wr_ref[...], preferred_element_type=F32) + gb_ref[...])
        merged = cab.astype(F32) + gate_c * cc.astype(F32)
        out = jnp.dot(merged.astype(BF16), wo_ref[...], preferred_element_type=F32)
        y = _layer_norm(ALPHA * h_ref[rows, :] + out, g_ref[...], b_ref[...])
        out_refs[0][rows, :] = y
        if len(out_refs) > 1:
            out_refs[1][rows, :] = y.astype(BF16)


def _merge_project_norm(contrib_ab, contrib_c, h_bf16, h, w_gate_c, w_o, layer, gate_bias_c, ln_g, ln_b, emit_bf16):
    s, d = h.shape
    tile = BRANCH_TILE
    n_tiles = s // tile
    row = pl.BlockSpec((2 * tile, d), lambda i: (i, 0))
    first = pl.BlockSpec((tile, d), lambda i: (2 * i + 1, 0))
    second = pl.BlockSpec((tile, d), lambda i: (2 * i + 2, 0))
    second_clamped = pl.BlockSpec((tile, d), lambda i: (jnp.minimum(2 * i + 2, n_tiles - 1), 0))
    out_specs = [row, row] if emit_bf16 else [row]
    out_shape = [jax.ShapeDtypeStruct((s, d), F32)] + ([jax.ShapeDtypeStruct((s, d), BF16)] if emit_bf16 else [])
    return pl.pallas_call(
        _final_kernel,
        grid=(s // (2 * tile),),
        in_specs=[first, second_clamped, _resident((tile, d)), first, second, row, row,
                  _layer_resident(layer, (d, d)), _layer_resident(layer, (d, d)),
                  _resident((1, d)), _resident((1, d)), _resident((1, d))],
        out_specs=out_specs,
        out_shape=out_shape,
        compiler_params=_params("parallel"),
        name="merge_project_norm",
    )(contrib_ab[0], contrib_ab[0], contrib_ab[1], contrib_c, contrib_c, h_bf16, h, w_gate_c, w_o,
      gate_bias_c.reshape(1, d), ln_g.reshape(1, d), ln_b.reshape(1, d))


def kernel(x, positions, ln0_g, ln0_b, w_in, conv_w, gmlp_ln_g, gmlp_ln_b, spatial_w, spatial_b, sink,
           w_branch_a, w_branch_b, w_branch_c, gate_b, w_out, ln_g, ln_b):
    bsz, s, d = x.shape
    assert (bsz, d) == (1, D_MODEL) and s % 1024 == 0 and w_in.shape == (DEPTH, D_MODEL, IN_WIDTH)

    def tiles(off, width):
        return tuple(range(off // WEIGHT_TILE, (off + width) // WEIGHT_TILE))

    w_conv = _gather_cast_columns(w_in, tiles(OFF_CONV, 4 * CONV_WIDTH) + tiles(OFF_GATE_A, d))
    w_gmlp = _gather_cast_columns(w_in, tiles(OFF_GMLP, 3 * GMLP_WIDTH) + tiles(OFF_GATE_B, d))
    w_qz = _gather_cast_columns(w_in, tiles(OFF_Q, ATTN_WIDTH) + tiles(OFF_ATTN_Z, ATTN_WIDTH))
    w_kv = _gather_cast_columns(w_in, tiles(OFF_KV, 2 * KV_WIDTH))
    w_gate_c = _gather_cast_columns(w_in, tiles(OFF_GATE_C, d))
    wa, wb, wc, wo = (_cast_bf16(w) for w in (w_branch_a, w_branch_b, w_branch_c, w_out))

    h = h_bf16 = cos_pad = sin_pad = None
    for l in range(DEPTH):
        if l == 0:
            kv_pad, h, h_bf16, cos_pad, sin_pad = _norm_kv_projection(
                x.reshape(s, d), ln0_g, ln0_b, positions.reshape(s), w_kv, l)
        else:
            kv_pad = _kv_projection(h_bf16, w_kv, l, cos_pad, sin_pad)
        contrib_a = _conv_branch(h_bf16, w_conv, wa, l, conv_w[l], gate_b[l, 0])
        contrib_ab = _gmlp_branch(h_bf16, w_gmlp, wb, l, gmlp_ln_g[l], gmlp_ln_b[l],
                                  spatial_w[l], spatial_b[l], gate_b[l, 1], accumulate_onto=contrib_a)
        contrib_c = _attention_branch(h_bf16, w_qz, wc, l, sink[l], kv_pad, cos_pad, sin_pad)
        outs = _merge_project_norm(contrib_ab, contrib_c, h_bf16, h, w_gate_c, wo, l, gate_b[l, 2],
                                   ln_g[l], ln_b[l], emit_bf16=l + 1 < DEPTH)
        h = outs[0]
        h_bf16 = outs[1] if l + 1 < DEPTH else None
    return h.reshape(bsz, s, d)
```
